```python
import math
import functools
import jax
import jax.numpy as jnp
from jax import lax
import numpy as np

D_MODEL = 1024
BATCH = 2
SEQ = 8192
DEPTH = 2
DEC_BATCH = 32
DEC_SEQ = 4
PAST_LEN = 16384
PAGE_SIZE = 128

RET_HEADS = 4
RET_DK = D_MODEL // RET_HEADS
RET_DV = 2 * D_MODEL // RET_HEADS
RET_CHUNK = 128
XPOS_BASE = 10000.0
GN_EPS = 1e-6
MOBA_HEADS = 8
MOBA_HD = D_MODEL // MOBA_HEADS
MOBA_WIDTH = MOBA_HEADS * MOBA_HD
MOBA_BLOCK = 256
MOBA_TOPK = 3
MOBA_Q_CHUNK = 64
REL_BUCKETS = 32
REL_MAX_DIST = 128
MOE_GROUPS = 4
MOE_EXPERTS_PER_GROUP = 8
MOE_EXPERTS = MOE_GROUPS * MOE_EXPERTS_PER_GROUP
MOE_TOPK = 2
MOE_FF = D_MODEL // 2
MOE_ROW_BLOCK = 64
DN_ALPHA = (2.0 * DEPTH) ** 0.25
DN_BETA = (8.0 * DEPTH) ** -0.25
LN_EPS = 1e-5
NEG_INF = -1e30

SPLIT_SIZES = (RET_HEADS * RET_DK, RET_HEADS * RET_DK, RET_HEADS * RET_DV, RET_HEADS * RET_DV,
               MOBA_WIDTH, MOBA_WIDTH, MOBA_WIDTH, D_MODEL, D_MODEL)
PROJ_WIDTH = sum(SPLIT_SIZES)

kernel_name = 'hybrid_retention_moba_hmoe_decode_step'


def layer_norm(x, g, b):
    xf = x.astype(jnp.float32)
    mu = xf.mean(-1, keepdims=True)
    xc = xf - mu
    var = (xc * xc).mean(-1, keepdims=True)
    return (xc * lax.rsqrt(var + LN_EPS) * g + b).astype(x.dtype)


def split_projection(x, w_in):
    b, s, _ = x.shape
    h = jnp.einsum('bsd,de->bse', x, w_in)
    cuts = [int(c) for c in np.cumsum(SPLIT_SIZES)[:-1]]
    rq, rk, rv, rg, mq, mk, mv, ga, gb = jnp.split(h, cuts, axis=-1)
    return (rq.reshape(b, s, RET_HEADS, RET_DK), rk.reshape(b, s, RET_HEADS, RET_DK),
            rv.reshape(b, s, RET_HEADS, RET_DV), rg,
            mq.reshape(b, s, MOBA_HEADS, MOBA_HD), mk.reshape(b, s, MOBA_HEADS, MOBA_HD),
            mv.reshape(b, s, MOBA_HEADS, MOBA_HD), ga, gb)


def xpos_rotate(x, pos):
    half = x.shape[-1] // 2
    inv = 1.0 / (XPOS_BASE ** jnp.linspace(0.0, 1.0, half, dtype=jnp.float32))
    ang = pos.astype(jnp.float32)[:, None] * inv[None, :]
    cos = jnp.cos(ang)[None, :, None, :]
    sin = jnp.sin(ang)[None, :, None, :]
    x1, x2 = x[..., :half], x[..., half:]
    return jnp.concatenate([x1 * cos - x2 * sin, x1 * sin + x2 * cos], axis=-1).astype(x.dtype)


def ret_log_decay():
    return jnp.log1p(-jnp.exp2(-5.0 - jnp.arange(RET_HEADS, dtype=jnp.float32)))


def retention_block(q, k, v, state):
    c = q.shape[2]
    lg = ret_log_decay()[:, None]
    j = jnp.arange(c, dtype=jnp.float32)
    diff = j[:, None] - j[None, :]
    dmask = jnp.where(diff >= 0, jnp.exp(lg[:, :, None] * jnp.maximum(diff, 0.0)), 0.0)
    qf, kf, vf = q.astype(jnp.float32), k.astype(jnp.float32), v.astype(jnp.float32)
    scores = jnp.einsum('bhid,bhjd->bhij', qf, kf) * dmask
    inner = jnp.einsum('bhij,bhje->bhie', scores, vf)
    cross = jnp.einsum('bhid,bhde->bhie', qf, state) * jnp.exp(lg * (j + 1.0))[..., None]
    k_dec = kf * jnp.exp(lg * (c - 1.0 - j))[..., None]
    new_state = jnp.exp(lg * c)[..., None] * state + jnp.einsum('bhjd,bhje->bhde', k_dec, vf)
    return inner + cross, new_state


def retention_prompt(q, k, v):
    b, s, h, dk = q.shape
    dv = v.shape[-1]
    nc = s // RET_CHUNK

    def to_chunks(t):
        return t.reshape(b, nc, RET_CHUNK, h, t.shape[-1]).transpose(1, 0, 3, 2, 4)

    def step(state, qkv):
        qc, kc, vc = qkv
        o, state = retention_block(qc, kc, vc, state)
        return state, o

    state0 = jnp.zeros((b, h, dk, dv), jnp.float32)
    final, o = lax.scan(step, state0, (to_chunks(q), to_chunks(k), to_chunks(v)))
    return o.transpose(1, 0, 3, 2, 4).reshape(b, s, h, dv), final


def retention_step(q, k, v, state):
    o, new_state = retention_block(q.transpose(0, 2, 1, 3), k.transpose(0, 2, 1, 3),
                                   v.transpose(0, 2, 1, 3), state.astype(jnp.float32))
    return o.transpose(0, 2, 1, 3), new_state


def retention_output(o, rg, gn_w):
    mu = o.mean(-1, keepdims=True)
    oc = o - mu
    var = (oc * oc).mean(-1, keepdims=True)
    on = (oc * lax.rsqrt(var + GN_EPS)).reshape(o.shape[0], o.shape[1], -1) * gn_w
    return (on * jax.nn.silu(rg.astype(jnp.float32))).astype(rg.dtype)


def rel_bucket(rel):
    n = jnp.maximum(rel, 0)
    exact = REL_BUCKETS // 2
    nf = jnp.maximum(n, 1).astype(jnp.float32)
    large = exact + (jnp.log(nf / exact) / math.log(REL_MAX_DIST / exact)
                     * (REL_BUCKETS - exact)).astype(jnp.int32)
    return jnp.where(n < exact, n, jnp.minimum(large, REL_BUCKETS - 1))


def moba_attend(q, q_pos, own_k, own_v, own_pos, sel, rel_bias):
    h = q.shape[1]
    scale = MOBA_HD ** -0.5
    bias_t = rel_bias.T.astype(jnp.float32)
    own_rel = q_pos[:, None] - own_pos[None, :]
    own_logit = (jnp.einsum('bhqd,bhkd->bhqk', q, own_k).astype(jnp.float32) * scale
                 + bias_t[jnp.arange(h)[:, None, None], rel_bucket(own_rel)[None]])
    own_logit = jnp.where(own_rel >= 0, own_logit, NEG_INF)
    if sel is None:
        p = jax.nn.softmax(own_logit, axis=-1)
        return jnp.einsum('bhqk,bhkd->bhqd', p, own_v)
    sel_k, sel_v, sel_pos, sel_valid = sel
    sel_rel = q_pos[:, None] - sel_pos
    sel_logit = (jnp.einsum('bhqd,bhqkd->bhqk', q, sel_k).astype(jnp.float32) * scale
                 + bias_t[jnp.arange(h)[None, :, None, None], rel_bucket(sel_rel)])
    sel_logit = jnp.where(sel_valid, sel_logit, NEG_INF)
    n_sel = sel_logit.shape[-1]
    p = jax.nn.softmax(jnp.concatenate([sel_logit, own_logit], axis=-1), axis=-1)
    return (jnp.einsum('bhqk,bhqkd->bhqd', p[..., :n_sel], sel_v)
            + jnp.einsum('bhqk,bhkd->bhqd', p[..., n_sel:], own_v))


def moba_prompt(q, k, v, rel_bias):
    b, s, h, d = q.shape
    nb = -(-s // MOBA_BLOCK)
    pad = nb * MOBA_BLOCK - s

    def blocks(t):
        t = jnp.pad(t, ((0, 0), (0, pad), (0, 0), (0, 0)))
        return t.reshape(b, nb, MOBA_BLOCK, h, d).transpose(0, 3, 1, 2, 4)

    kb, vb = blocks(k), blocks(v)
    qh = q.transpose(0, 2, 1, 3)
    nq = s // MOBA_Q_CHUNK
    n_sel = min(MOBA_TOPK, nb - 1)
    q_block = jnp.arange(s) // MOBA_BLOCK
    if n_sel > 0:
        means = kb.astype(jnp.float32).mean(axis=3)
        score = jnp.einsum('bhsd,bhnd->bhsn', qh.astype(jnp.float32), means)
        score = jnp.where(jnp.arange(nb)[None, :] < q_block[:, None], score, NEG_INF)
        _, idx = lax.top_k(score, n_sel)
    else:
        idx = jnp.zeros((b, h, s, 0), jnp.int32)
    q_chunks = qh.reshape(b, h, nq, MOBA_Q_CHUNK, d).transpose(2, 0, 1, 3, 4)
    idx_chunks = idx.reshape(b, h, nq, MOBA_Q_CHUNK, n_sel).transpose(2, 0, 1, 3, 4)
    b_ix = jnp.arange(b)[:, None, None, None]
    h_ix = jnp.arange(h)[None, :, None, None]

    def chunk(args):
        c, qc, ic = args
        q_pos = c * MOBA_Q_CHUNK + jnp.arange(MOBA_Q_CHUNK)
        ob = (c * MOBA_Q_CHUNK) // MOBA_BLOCK
        own_k = lax.dynamic_index_in_dim(kb, ob, axis=2, keepdims=False)
        own_v = lax.dynamic_index_in_dim(vb, ob, axis=2, keepdims=False)
        own_pos = ob * MOBA_BLOCK + jnp.arange(MOBA_BLOCK)
        sel = None
        if n_sel > 0:
            sk = kb[b_ix, h_ix, ic].reshape(b, h, MOBA_Q_CHUNK, n_sel * MOBA_BLOCK, d)
            sv = vb[b_ix, h_ix, ic].reshape(b, h, MOBA_Q_CHUNK, n_sel * MOBA_BLOCK, d)
            sp = (ic[..., None] * MOBA_BLOCK + jnp.arange(MOBA_BLOCK)).reshape(b, h, MOBA_Q_CHUNK, -1)
            valid = jnp.repeat(jnp.arange(n_sel)[None, :] < (q_pos // MOBA_BLOCK)[:, None],
                               MOBA_BLOCK, axis=1)
            sel = (sk, sv, sp, valid)
        return moba_attend(qc, q_pos, own_k, own_v, own_pos, sel, rel_bias)

    out = lax.map(chunk, (jnp.arange(nq), q_chunks, idx_chunks))
    return out.transpose(1, 0, 3, 2, 4).reshape(b, s, h, d)


def moba_sample(q, k, v, cache_k, cache_v, layer, page_table, rel_bias):
    db, t, h, d = q.shape
    n_pages = page_table.shape[1]
    past = n_pages * PAGE_SIZE
    ppb = MOBA_BLOCK // PAGE_SIZE
    n_full = past // MOBA_BLOCK
    q_pos = past + jnp.arange(t)
    qh = q.transpose(0, 2, 1, 3)
    own_k, own_v = k, v
    if n_pages > n_full * ppb:
        own_pages = page_table[:, n_full * ppb:]
        pk = cache_k[layer, own_pages].reshape(db, -1, h, d).astype(k.dtype)
        pv = cache_v[layer, own_pages].reshape(db, -1, h, d).astype(v.dtype)
        own_k = jnp.concatenate([pk, k], axis=1)
        own_v = jnp.concatenate([pv, v], axis=1)
    own_pos = n_full * MOBA_BLOCK + jnp.arange(own_k.shape[1])
    sel = None
    n_sel = min(MOBA_TOPK, n_full)
    if n_sel > 0:
        full_pages = page_table[:, :n_full * ppb]
        means = cache_k[layer, full_pages].astype(jnp.float32).reshape(
            db, n_full, MOBA_BLOCK, h, d).mean(axis=2)
        score = jnp.einsum('bhtd,bnhd->bhtn', qh.astype(jnp.float32), means)
        _, idx = lax.top_k(score, n_sel)
        phys = full_pages.reshape(db, n_full, ppb)[jnp.arange(db)[:, None, None, None], idx]
        rows = jnp.arange(PAGE_SIZE)
        h_ix = jnp.arange(h)[None, :, None, None, None, None]
        sk = cache_k[layer, phys[..., None], rows, h_ix].reshape(db, h, t, n_sel * MOBA_BLOCK, d)
        sv = cache_v[layer, phys[..., None], rows, h_ix].reshape(db, h, t, n_sel * MOBA_BLOCK, d)
        sp = (idx[..., None] * MOBA_BLOCK + jnp.arange(MOBA_BLOCK)).reshape(db, h, t, -1)
        sel = (sk.astype(q.dtype), sv.astype(q.dtype), sp, jnp.ones((1, 1), bool))
    out = moba_attend(qh, q_pos, own_k.transpose(0, 2, 1, 3), own_v.transpose(0, 2, 1, 3),
                      own_pos, sel, rel_bias)
    return out.transpose(0, 2, 1, 3)


def moe_ffn(x, w_group, w_router, w_exp_gate, w_exp_up, w_exp_down):
    shape = x.shape
    xt = x.reshape(-1, shape[-1])
    n = xt.shape[0]
    p_group = jax.nn.softmax(jnp.dot(xt, w_group).astype(jnp.float32), axis=-1)
    grp = jnp.argmax(p_group, axis=-1)
    logits = jnp.dot(xt, w_router).astype(jnp.float32).reshape(n, MOE_GROUPS, MOE_EXPERTS_PER_GROUP)
    logits = jnp.take_along_axis(logits, grp[:, None, None], axis=1)[:, 0]
    top_val, top_idx = lax.top_k(logits, MOE_TOPK)
    gate = jax.nn.softmax(top_val, axis=-1) * jnp.take_along_axis(p_group, grp[:, None], axis=1)
    eid = (grp[:, None] * MOE_EXPERTS_PER_GROUP + top_idx).reshape(-1).astype(jnp.int32)
    tok = jnp.repeat(jnp.arange(n, dtype=jnp.int32), MOE_TOPK)
    wt = gate.reshape(-1)
    order = jnp.argsort(eid)
    s_eid, s_tok, s_wt = eid[order], tok[order], wt[order]
    counts = jnp.bincount(eid, length=MOE_EXPERTS)
    padded = (counts + MOE_ROW_BLOCK - 1) // MOE_ROW_BLOCK * MOE_ROW_BLOCK
    pad_end = jnp.cumsum(padded)
    pad_start = pad_end - padded
    start = jnp.cumsum(counts) - counts
    n_assign = eid.shape[0]
    dest = pad_start[s_eid] + jnp.arange(n_assign) - start[s_eid]
    n_blk = -(-n_assign // MOE_ROW_BLOCK) + MOE_EXPERTS
    slot_tok = jnp.full((n_blk * MOE_ROW_BLOCK,), n, jnp.int32).at[dest].set(s_tok)
    blk_exp = jnp.minimum(jnp.searchsorted(pad_end, jnp.arange(n_blk) * MOE_ROW_BLOCK, side='right'),
                          MOE_EXPERTS - 1)
    x_rows = jnp.concatenate([xt, jnp.zeros((1, xt.shape[1]), xt.dtype)], axis=0)[slot_tok]
    x_rows = x_rows.reshape(n_blk, MOE_ROW_BLOCK, -1)

    def expert_rows(args):
        xe, e = args
        hid = jax.nn.silu(xe @ w_exp_gate[e]) * (xe @ w_exp_up[e])
        return hid @ w_exp_down[e]

    y_rows = lax.map(expert_rows, (x_rows, blk_exp)).reshape(n_blk * MOE_ROW_BLOCK, -1)
    contrib = (s_wt[:, None] * y_rows[dest]).astype(xt.dtype)
    y = jnp.zeros_like(xt).at[s_tok].add(contrib)
    return y.reshape(shape)


def decoder_layer(x, pos, ret_fn, moba_fn, w_in, ret_gn_w, w_ret_o, w_moba_o, w_out, ln1_g, ln1_b,
                  w_group, w_router, w_exp_gate, w_exp_up, w_exp_down, ln2_g, ln2_b):
    b, s, _ = x.shape
    rq, rk, rv, rg, mq, mk, mv, ga, gb = split_projection(x, w_in)
    rq = xpos_rotate(rq, pos)
    rk = xpos_rotate(rk, pos) * (RET_DK ** -0.5)
    o_ret, ret_state = ret_fn(rq, rk, rv)
    o_moba = moba_fn(mq, mk, mv)
    ret_branch = jnp.dot(retention_output(o_ret, rg, ret_gn_w), w_ret_o)
    moba_branch = jnp.dot(o_moba.reshape(b, s, MOBA_WIDTH).astype(x.dtype), w_moba_o)
    mixed = jnp.dot(jax.nn.sigmoid(ga) * ret_branch + jax.nn.sigmoid(gb) * moba_branch, w_out)
    x = layer_norm(DN_ALPHA * x + mixed, ln1_g, ln1_b)
    x = layer_norm(DN_ALPHA * x + moe_ffn(x, w_group, w_router, w_exp_gate, w_exp_up, w_exp_down),
                   ln2_g, ln2_b)
    return x, mk, mv, ret_state


def setup_inputs(seed: int = 0) -> dict:
    key = jax.random.key(seed)
    ks = jax.random.split(key, 24)
    f32 = jnp.float32
    n_pages = PAST_LEN // PAGE_SIZE
    n_used = DEC_BATCH * n_pages
    n_phys = n_used + n_used // 4

    def nrm(k, shape, scale):
        return jax.random.normal(k, shape, f32) * scale

    ret_w = RET_HEADS * RET_DV
    return {
        'x_prompt': nrm(ks[0], (BATCH, SEQ, D_MODEL), 1.0),
        'x_sample': nrm(ks[1], (DEC_BATCH, DEC_SEQ, D_MODEL), 1.0),
        'cache_k': nrm(ks[2], (DEPTH, n_phys, PAGE_SIZE, MOBA_HEADS, MOBA_HD), 1.0),
        'cache_v': nrm(ks[3], (DEPTH, n_phys, PAGE_SIZE, MOBA_HEADS, MOBA_HD), 1.0),
        'state_ret': nrm(ks[4], (DEPTH, DEC_BATCH, RET_HEADS, RET_DK, RET_DV), 0.3),
        'page_table': jax.random.permutation(ks[5], n_phys)[:n_used].reshape(DEC_BATCH, n_pages).astype(jnp.int32),
        'rel_bias': nrm(ks[6], (REL_BUCKETS, MOBA_HEADS), 0.5),
        'w_in': nrm(ks[7], (DEPTH, D_MODEL, PROJ_WIDTH), D_MODEL ** -0.5),
        'ret_gn_w': 1.0 + nrm(ks[8], (DEPTH, ret_w), 0.02),
        'w_ret_o': nrm(ks[9], (DEPTH, ret_w, D_MODEL), ret_w ** -0.5),
        'w_moba_o': nrm(ks[10], (DEPTH, MOBA_WIDTH, D_MODEL), MOBA_WIDTH ** -0.5),
        'w_out': nrm(ks[11], (DEPTH, D_MODEL, D_MODEL), D_MODEL ** -0.5 * DN_BETA),
        'ln1_g': 1.0 + nrm(ks[12], (DEPTH, D_MODEL), 0.02),
        'ln1_b': nrm(ks[13], (DEPTH, D_MODEL), 0.02),
        'w_group': nrm(ks[14], (DEPTH, D_MODEL, MOE_GROUPS), D_MODEL ** -0.5),
        'w_router': nrm(ks[15], (DEPTH, D_MODEL, MOE_EXPERTS), D_MODEL ** -0.5),
        'w_exp_gate': nrm(ks[16], (DEPTH, MOE_EXPERTS, D_MODEL, MOE_FF), D_MODEL ** -0.5),
        'w_exp_up': nrm(ks[17], (DEPTH, MOE_EXPERTS, D_MODEL, MOE_FF), D_MODEL ** -0.5),
        'w_exp_down': nrm(ks[18], (DEPTH, MOE_EXPERTS, MOE_FF, D_MODEL), MOE_FF ** -0.5 * DN_BETA),
        'ln2_g': 1.0 + nrm(ks[19], (DEPTH, D_MODEL), 0.02),
        'ln2_b': nrm(ks[20], (DEPTH, D_MODEL), 0.02),
    }


def reference(x_prompt, x_sample, cache_k, cache_v, state_ret, page_table, rel_bias, w_in, ret_gn_w,
              w_ret_o, w_moba_o, w_out, ln1_g, ln1_b, w_group, w_router, w_exp_gate, w_exp_up,
              w_exp_down, ln2_g, ln2_b):
    b, s, _ = x_prompt.shape
    pos_p = jnp.arange(s)
    past = page_table.shape[1] * PAGE_SIZE
    pos_s = past + jnp.arange(x_sample.shape[1])
    yp, ys = x_prompt, x_sample
    kp, vp, rp, kss, vss, rss = [], [], [], [], [], []
    for l in range(DEPTH):
        lw = (w_in[l], ret_gn_w[l], w_ret_o[l], w_moba_o[l], w_out[l], ln1_g[l], ln1_b[l],
              w_group[l], w_router[l], w_exp_gate[l], w_exp_up[l], w_exp_down[l], ln2_g[l], ln2_b[l])
        yp, k_l, v_l, st_l = decoder_layer(
            yp, pos_p, retention_prompt, functools.partial(moba_prompt, rel_bias=rel_bias), *lw)
        kp.append(k_l.reshape(b, s // PAGE_SIZE, PAGE_SIZE, MOBA_HEADS, MOBA_HD))
        vp.append(v_l.reshape(b, s // PAGE_SIZE, PAGE_SIZE, MOBA_HEADS, MOBA_HD))
        rp.append(st_l)
        ys, k_s, v_s, st_s = decoder_layer(
            ys, pos_s, functools.partial(retention_step, state=state_ret[l]),
            functools.partial(moba_sample, cache_k=cache_k, cache_v=cache_v, layer=l,
                              page_table=page_table, rel_bias=rel_bias), *lw)
        kss.append(k_s)
        vss.append(v_s)
        rss.append(st_s)
    return (yp, ys, jnp.stack(kp), jnp.stack(vp), jnp.stack(rp), jnp.stack(kss), jnp.stack(vss), jnp.stack(rss))
```

```python
import functools
import math

import numpy as np
import jax
import jax.numpy as jnp
from jax import lax
from jax.experimental import pallas as pl
from jax.experimental.pallas import tpu as pltpu

D_MODEL = 1024
DEPTH = 2
PAGE_SIZE = 128
RET_HEADS = 4
RET_DK = D_MODEL // RET_HEADS
RET_DV = 2 * D_MODEL // RET_HEADS
RET_W = RET_HEADS * RET_DV
RET_CHUNK = 128
XPOS_BASE = 10000.0
GN_EPS = 1e-6
MOBA_HEADS = 8
MOBA_HD = D_MODEL // MOBA_HEADS
MOBA_BLOCK = 256
MOBA_TOPK = 3
REL_BUCKETS = 32
REL_MAX_DIST = 128
MOE_GROUPS = 4
MOE_EPG = 8
MOE_EXPERTS = MOE_GROUPS * MOE_EPG
MOE_FF = D_MODEL // 2
DN_ALPHA = (2.0 * DEPTH) ** 0.25
LN_EPS = 1e-5
NEG_INF = -1e30

COL_RQ, COL_RV, COL_RG, COL_MQ, COL_MK, COL_GA = 0, 2, 4, 6, 7, 9
PROJ_TILES = 11

LANES = 128
SUBLANES = 8
VMEM_LIMIT = 56 * 1024 * 1024
SAMPLE_ROWS = 8
MOE_ROWS = 256
PAGES_PER_STEP = 8


def _cparams(sem):
    return pltpu.CompilerParams(dimension_semantics=sem, vmem_limit_bytes=VMEM_LIMIT)


def _sigmoid(x):
    return 1.0 / (1.0 + jnp.exp(-x))


def _proj_kernel(x_ref, w_ref, s_ref, o_ref, wb_ref):
    @pl.when(pl.program_id(1) == 0)
    def _():
        wb_ref[...] = w_ref[...].astype(jnp.bfloat16)

    acc = jnp.dot(x_ref[...], wb_ref[...], preferred_element_type=jnp.float32)
    o_ref[...] = (acc * s_ref[...]).astype(o_ref.dtype)


def _proj(xb, w_in, layer, col_scale, col0, ncols, out_dtype, tm):
    n = xb.shape[0]
    tn = D_MODEL
    return pl.pallas_call(
        _proj_kernel,
        out_shape=jax.ShapeDtypeStruct((n, ncols * tn), out_dtype),
        grid=(ncols, n // tm),
        in_specs=[
            pl.BlockSpec((tm, D_MODEL), lambda j, i: (i, 0)),
            pl.BlockSpec((None, D_MODEL, tn), lambda j, i: (layer, 0, j + col0)),
            pl.BlockSpec((1, tn), lambda j, i: (0, j + col0)),
        ],
        out_specs=pl.BlockSpec((tm, tn), lambda j, i: (i, j)),
        scratch_shapes=[pltpu.VMEM((D_MODEL, tn), jnp.bfloat16)],
        compiler_params=_cparams(("arbitrary", "arbitrary")),
        name="in_proj",
    )(xb, w_in, col_scale)


def _ret_consts(c, c_true):
    lg = np.log1p(-np.exp2(-5.0 - np.arange(RET_HEADS, dtype=np.float64)))
    j = np.arange(c, dtype=np.float64)
    diff = j[:, None] - j[None, :]
    dmask = np.where(diff >= 0, np.exp(lg[:, None, None] * np.maximum(diff, 0.0)), 0.0)
    qdec = np.exp(lg[:, None] * (j + 1.0))[..., None]
    kdec = np.where(j < c_true, np.exp(lg[:, None] * (c_true - 1.0 - j)), 0.0)[..., None]
    sdec = [float(v) for v in np.exp(lg * c_true)]
    return (jnp.asarray(dmask, jnp.float32), jnp.asarray(qdec, jnp.float32),
            jnp.asarray(kdec, jnp.float32), sdec)


def _ret_kernel(sdec, q_ref, k_ref, v_ref, g_ref, cos_ref, sin_ref, dm_ref, qd_ref, kd_ref,
                gn_ref, s0_ref, o_ref, sf_ref, st_ref):
    c = pl.program_id(1)

    @pl.when(c == 0)
    def _():
        st_ref[...] = s0_ref[0]

    cos = cos_ref[...]
    sin = sin_ref[...]
    half = RET_DK // 2

    def rot(x):
        x1, x2 = x[:, :half], x[:, half:]
        return jnp.concatenate([x1 * cos - x2 * sin, x1 * sin + x2 * cos], axis=-1)

    for h in range(RET_HEADS):
        qr = rot(q_ref[:, h * RET_DK:(h + 1) * RET_DK])
        kr = rot(k_ref[:, h * RET_DK:(h + 1) * RET_DK])
        qb = qr.astype(jnp.bfloat16)
        kb = kr.astype(jnp.bfloat16)
        v = v_ref[:, h * RET_DV:(h + 1) * RET_DV].astype(jnp.bfloat16)
        st = st_ref[h]
        s = lax.dot_general(qb, kb, (((1,), (1,)), ((), ())),
                            preferred_element_type=jnp.float32) * dm_ref[h]
        inner = jnp.dot(s.astype(jnp.bfloat16), v, preferred_element_type=jnp.float32)
        cross = jnp.dot(qb, st.astype(jnp.bfloat16),
                        preferred_element_type=jnp.float32) * qd_ref[h]
        kdb = (kr * kd_ref[h]).astype(jnp.bfloat16)
        st_ref[h] = sdec[h] * st + lax.dot_general(
            kdb, v, (((0,), (0,)), ((), ())), preferred_element_type=jnp.float32)
        o = inner + cross
        mu = jnp.mean(o, axis=-1, keepdims=True)
        oc = o - mu
        var = jnp.mean(oc * oc, axis=-1, keepdims=True)
        on = oc * lax.rsqrt(var + GN_EPS) * gn_ref[:, h * RET_DV:(h + 1) * RET_DV]
        g = g_ref[:, h * RET_DV:(h + 1) * RET_DV].astype(jnp.float32)
        o_ref[:, h * RET_DV:(h + 1) * RET_DV] = (on * (g * _sigmoid(g))).astype(o_ref.dtype)

    @pl.when(c == pl.num_programs(1) - 1)
    def _():
        sf_ref[0] = st_ref[...]


def _retention(qk, vg, v_blk, g_blk, cos, sin, gn_w, state0, s0_off, nb, nc, c, c_true,
               out_dtype):
    n = qk.shape[0]
    dmask, qdec, kdec, sdec = _ret_consts(c, c_true)
    row = lambda b, i: b * nc + i
    return pl.pallas_call(
        functools.partial(_ret_kernel, sdec),
        out_shape=(jax.ShapeDtypeStruct((n, RET_W), out_dtype),
                   jax.ShapeDtypeStruct((nb, RET_HEADS, RET_DK, RET_DV), jnp.float32)),
        grid=(nb, nc),
        in_specs=[
            pl.BlockSpec((c, D_MODEL), lambda b, i: (row(b, i), 0)),
            pl.BlockSpec((c, D_MODEL), lambda b, i: (row(b, i), 1)),
            pl.BlockSpec((c, RET_W), lambda b, i: (row(b, i), v_blk)),
            pl.BlockSpec((c, RET_W), lambda b, i: (row(b, i), g_blk)),
            pl.BlockSpec((c, RET_DK // 2), lambda b, i: (i, 0)),
            pl.BlockSpec((c, RET_DK // 2), lambda b, i: (i, 0)),
            pl.BlockSpec((RET_HEADS, c, c), lambda b, i: (0, 0, 0)),
            pl.BlockSpec((RET_HEADS, c, 1), lambda b, i: (0, 0, 0)),
            pl.BlockSpec((RET_HEADS, c, 1), lambda b, i: (0, 0, 0)),
            pl.BlockSpec((1, RET_W), lambda b, i: (0, 0)),
            pl.BlockSpec((1, RET_HEADS, RET_DK, RET_DV), lambda b, i: (s0_off + b, 0, 0, 0)),
        ],
        out_specs=(pl.BlockSpec((c, RET_W), lambda b, i: (row(b, i), 0)),
                   pl.BlockSpec((1, RET_HEADS, RET_DK, RET_DV), lambda b, i: (b, 0, 0, 0))),
        scratch_shapes=[pltpu.VMEM((RET_HEADS, RET_DK, RET_DV), jnp.float32)],
        compiler_params=_cparams(("arbitrary", "arbitrary")),
        name="retention",
    )(qk, qk, vg, vg, cos, sin, dmask, qdec, kdec, gn_w, state0)


def _xpos_tables(pos):
    half = RET_DK // 2
    inv = 1.0 / (XPOS_BASE ** jnp.linspace(0.0, 1.0, half, dtype=jnp.float32))
    ang = pos.astype(jnp.float32)[:, None] * inv[None, :]
    return jnp.cos(ang), jnp.sin(ang)


def _rel_bucket_np(rel):
    n = np.maximum(rel, 0)
    exact = REL_BUCKETS // 2
    nf = np.maximum(n, 1).astype(np.float32)
    large = exact + (np.log(nf / np.float32(exact)) / np.float32(math.log(REL_MAX_DIST / exact))
                     * np.float32(REL_BUCKETS - exact)).astype(np.int32)
    return np.where(n < exact, n, np.minimum(large, REL_BUCKETS - 1)).astype(np.int32)


def _bias_from_buckets(bucket, table_ref, h):
    out = jnp.zeros(bucket.shape, jnp.float32)
    for b in range(REL_BUCKETS):
        out = jnp.where(bucket == b, table_ref[h * REL_BUCKETS + b], out)
    return out


def _top3_mask(scores, valid, col, ncol):
    sc = jnp.where(valid, scores, NEG_INF)
    sel = jnp.zeros(scores.shape, jnp.float32)
    for _ in range(MOBA_TOPK):
        m = jnp.max(sc, axis=1, keepdims=True)
        idx = jnp.min(jnp.where(sc == m, col, ncol), axis=1, keepdims=True)
        pick = col == idx
        sel = jnp.where(pick, 1.0, sel)
        sc = jnp.where(pick, -jnp.inf, sc)
    return jnp.where(valid, sel, 0.0)


def _moba_prompt_kernel(nblk, tbl_ref, q_ref, k_ref, v_ref, bk_ref, o_ref,
                        kb_ref, vb_ref, mh_ref, ml_ref, bown_ref, bprev_ref, sel_ref,
                        m_ref, l_ref, acc_ref):
    h = pl.program_id(1)
    qi = pl.program_id(2)
    blk = MOBA_BLOCK

    @pl.when(qi == 0)
    def _():
        kf = k_ref[...]
        kb_ref[...] = kf.astype(jnp.bfloat16)
        vb_ref[...] = v_ref[...].astype(jnp.bfloat16)
        means = jnp.mean(kf.reshape(nblk, blk, MOBA_HD), axis=1)
        mh = means.astype(jnp.bfloat16)
        mh_ref[...] = mh
        ml_ref[...] = (means - mh.astype(jnp.float32)).astype(jnp.bfloat16)
        bown_ref[...] = _bias_from_buckets(bk_ref[0], tbl_ref, h)
        bprev_ref[...] = _bias_from_buckets(bk_ref[1], tbl_ref, h)

    q = q_ref[...]
    nt = (((1,), (1,)), ((), ()))
    scores = (lax.dot_general(q, mh_ref[...], nt, preferred_element_type=jnp.float32)
              + lax.dot_general(q, ml_ref[...], nt, preferred_element_type=jnp.float32))
    col = lax.broadcasted_iota(jnp.int32, scores.shape, 1)
    sel_ref[...] = _top3_mask(scores, col < qi, col, nblk)

    own = pl.multiple_of(qi * blk, blk)
    s = lax.dot_general(q, kb_ref[pl.ds(own, blk), :], nt,
                        preferred_element_type=jnp.float32) + bown_ref[...]
    r_i = lax.broadcasted_iota(jnp.int32, s.shape, 0)
    c_i = lax.broadcasted_iota(jnp.int32, s.shape, 1)
    s = jnp.where(r_i >= c_i, s, NEG_INF)
    m0 = jnp.max(s, axis=1, keepdims=True)
    p = jnp.exp(s - m0)
    m_ref[...] = m0
    l_ref[...] = jnp.sum(p, axis=1, keepdims=True)
    acc_ref[...] = jnp.dot(p.astype(jnp.bfloat16), vb_ref[pl.ds(own, blk), :],
                           preferred_element_type=jnp.float32)

    far_bias = tbl_ref[h * REL_BUCKETS + REL_BUCKETS - 1]

    def visit(n, bias):
        start = pl.multiple_of(n * blk, blk)
        picked = jnp.sum(jnp.where(col == n, sel_ref[...], 0.0), axis=1, keepdims=True) > 0.0
        sn = lax.dot_general(q, kb_ref[pl.ds(start, blk), :], nt,
                             preferred_element_type=jnp.float32) + bias
        sn = jnp.where(picked, sn, NEG_INF)
        m_old = m_ref[...]
        m_new = jnp.maximum(m_old, jnp.max(sn, axis=1, keepdims=True))
        a = jnp.exp(m_old - m_new)
        pn = jnp.exp(sn - m_new)
        l_ref[...] = a * l_ref[...] + jnp.sum(pn, axis=1, keepdims=True)
        acc_ref[...] = a * acc_ref[...] + jnp.dot(
            pn.astype(jnp.bfloat16), vb_ref[pl.ds(start, blk), :],
            preferred_element_type=jnp.float32)
        m_ref[...] = m_new

    def far(n, carry):
        visit(n, far_bias)
        return carry

    lax.fori_loop(0, jnp.maximum(qi - 1, 0), far, 0)

    @pl.when(qi >= 1)
    def _():
        visit(qi - 1, bprev_ref[...])

    o_ref[...] = (acc_ref[...] / l_ref[...]).astype(o_ref.dtype)


def _moba_prompt(q_arr, q_blk0, kv, rel_tbl, nb, s, out_dtype):
    n = kv.shape[0]
    blk = MOBA_BLOCK
    nblk = s // blk
    i = np.arange(blk)
    buckets = jnp.asarray(np.stack([_rel_bucket_np(i[:, None] - i[None, :]),
                                    _rel_bucket_np(blk + i[:, None] - i[None, :])]))
    grid_spec = pltpu.PrefetchScalarGridSpec(
        num_scalar_prefetch=1,
        grid=(nb, MOBA_HEADS, nblk),
        in_specs=[
            pl.BlockSpec((blk, MOBA_HD), lambda b, h, qi, t: (b * nblk + qi, q_blk0 + h)),
            pl.BlockSpec((s, MOBA_HD), lambda b, h, qi, t: (b, h)),
            pl.BlockSpec((s, MOBA_HD), lambda b, h, qi, t: (b, MOBA_HEADS + h)),
            pl.BlockSpec((2, blk, blk), lambda b, h, qi, t: (0, 0, 0)),
        ],
        out_specs=pl.BlockSpec((blk, MOBA_HD), lambda b, h, qi, t: (b * nblk + qi, h)),
        scratch_shapes=[
            pltpu.VMEM((s, MOBA_HD), jnp.bfloat16),
            pltpu.VMEM((s, MOBA_HD), jnp.bfloat16),
            pltpu.VMEM((nblk, MOBA_HD), jnp.bfloat16),
            pltpu.VMEM((nblk, MOBA_HD), jnp.bfloat16),
            pltpu.VMEM((blk, blk), jnp.float32),
            pltpu.VMEM((blk, blk), jnp.float32),
            pltpu.VMEM((blk, nblk), jnp.float32),
            pltpu.VMEM((blk, 1), jnp.float32),
            pltpu.VMEM((blk, 1), jnp.float32),
            pltpu.VMEM((blk, MOBA_HD), jnp.float32),
        ],
    )
    return pl.pallas_call(
        functools.partial(_moba_prompt_kernel, nblk),
        out_shape=jax.ShapeDtypeStruct((n, D_MODEL), out_dtype),
        grid_spec=grid_spec,
        compiler_params=_cparams(("arbitrary", "arbitrary", "arbitrary")),
        name="moba_prompt",
    )(rel_tbl, q_arr, kv, kv, buckets)


def _page_sum_kernel(c_ref, o_ref):
    o_ref[0] = jnp.sum(c_ref[0], axis=1)


def _page_sums(cache_k):
    nl, npg = cache_k.shape[0], cache_k.shape[1]
    pb = PAGES_PER_STEP
    return pl.pallas_call(
        _page_sum_kernel,
        out_shape=jax.ShapeDtypeStruct((nl, npg, MOBA_HEADS, MOBA_HD), jnp.float32),
        grid=(nl, npg // pb),
        in_specs=[pl.BlockSpec((1, pb, PAGE_SIZE, MOBA_HEADS, MOBA_HD),
                               lambda l, i: (l, i, 0, 0, 0))],
        out_specs=pl.BlockSpec((1, pb, MOBA_HEADS, MOBA_HD), lambda l, i: (l, i, 0, 0)),
        compiler_params=_cparams(("arbitrary", "arbitrary")),
        name="page_sums",
    )(cache_k)


def _sample_select_kernel(n_pages, layer, pt_ref, q_ref, ps_hbm, o_ref, ps_ref, sem):
    db = pl.program_id(0)
    nfull = n_pages // 2

    def page_copy(j):
        dst = (j % 2) * nfull + j // 2
        return pltpu.make_async_copy(ps_hbm.at[layer, pt_ref[db * n_pages + j]],
                                     ps_ref.at[dst], sem)

    def start(j, carry):
        page_copy(j).start()
        return carry

    def wait(j, carry):
        page_copy(j).wait()
        return carry

    lax.fori_loop(0, n_pages, start, 0)
    lax.fori_loop(0, n_pages, wait, 0)

    nt = (((1,), (1,)), ((), ()))
    for h in range(MOBA_HEADS):
        means = (ps_ref[pl.ds(0, nfull), h, :] + ps_ref[pl.ds(nfull, nfull), h, :]) * (
            1.0 / MOBA_BLOCK)
        mh = means.astype(jnp.bfloat16)
        ml = (means - mh.astype(jnp.float32)).astype(jnp.bfloat16)
        q = q_ref[:, h * MOBA_HD:(h + 1) * MOBA_HD].astype(jnp.bfloat16)
        scores = (lax.dot_general(q, mh, nt, preferred_element_type=jnp.float32)
                  + lax.dot_general(q, ml, nt, preferred_element_type=jnp.float32))
        col = lax.broadcasted_iota(jnp.int32, scores.shape, 1)
        sc = scores
        out = jnp.zeros((SAMPLE_ROWS, LANES), jnp.int32)
        lane = lax.broadcasted_iota(jnp.int32, out.shape, 1)
        for r in range(MOBA_TOPK):
            m = jnp.max(sc, axis=1, keepdims=True)
            idx = jnp.min(jnp.where(sc == m, col, nfull), axis=1, keepdims=True)
            out = jnp.where(lane == r, idx, out)
            sc = jnp.where(col == idx, -jnp.inf, sc)
        o_ref[0, h] = out


def _sample_select(page_table, q_arr, q_blk0, psums, layer):
    ndb, n_pages = page_table.shape
    grid_spec = pltpu.PrefetchScalarGridSpec(
        num_scalar_prefetch=1,
        grid=(ndb,),
        in_specs=[
            pl.BlockSpec((SAMPLE_ROWS, D_MODEL), lambda d, pt: (d, q_blk0)),
            pl.BlockSpec(memory_space=pl.ANY),
        ],
        out_specs=pl.BlockSpec((1, MOBA_HEADS, SAMPLE_ROWS, LANES), lambda d, pt: (d, 0, 0, 0)),
        scratch_shapes=[pltpu.VMEM((n_pages, MOBA_HEADS, MOBA_HD), jnp.float32),
                        pltpu.SemaphoreType.DMA],
    )
    return pl.pallas_call(
        functools.partial(_sample_select_kernel, n_pages, layer),
        out_shape=jax.ShapeDtypeStruct((ndb, MOBA_HEADS, SAMPLE_ROWS, LANES), jnp.int32),
        grid_spec=grid_spec,
        compiler_params=_cparams(("arbitrary",)),
        name="sample_select",
    )(page_table.reshape(-1), q_arr, psums)


def _sample_attn_kernel(n_pages, t_real, layer, pt_ref, idx_ref, tbl_ref, q_ref, k_ref, v_ref,
                        bk_ref, ck_hbm, cv_hbm, o_ref, ks_ref, vs_ref, sem):
    db = pl.program_id(0)
    h = pl.program_id(1)
    ppb = MOBA_BLOCK // PAGE_SIZE
    nslot = MOBA_TOPK * ppb

    def copies(t, j):
        blk = idx_ref[((db * MOBA_HEADS + h) * SAMPLE_ROWS + t) * MOBA_TOPK + j // ppb]
        page = pt_ref[db * n_pages + blk * ppb + j % ppb]
        rows = pl.ds(j * PAGE_SIZE, PAGE_SIZE)
        return (pltpu.make_async_copy(ck_hbm.at[layer, page, :, h, :], ks_ref.at[t, rows], sem.at[0]),
                pltpu.make_async_copy(cv_hbm.at[layer, page, :, h, :], vs_ref.at[t, rows], sem.at[1]))

    for t in range(t_real):
        for j in range(nslot):
            ck, cv = copies(t, j)
            ck.start()
            cv.start()
    for t in range(t_real):
        for j in range(nslot):
            ck, cv = copies(t, j)
            ck.wait()
            cv.wait()

    nt = (((1,), (1,)), ((), ()))
    bf16 = jnp.bfloat16
    own_w = 2 * SAMPLE_ROWS
    q = q_ref[...].astype(bf16)
    pad = jnp.zeros((own_w - SAMPLE_ROWS, MOBA_HD), jnp.float32)
    kn = jnp.concatenate([k_ref[...], pad], axis=0).astype(bf16)
    vn = jnp.concatenate([v_ref[...], pad], axis=0).astype(bf16)
    far_bias = tbl_ref[h * REL_BUCKETS + REL_BUCKETS - 1]
    last_blk = n_pages // ppb - 1
    r_i = lax.broadcasted_iota(jnp.int32, (SAMPLE_ROWS, own_w), 0)
    c_i = lax.broadcasted_iota(jnp.int32, (SAMPLE_ROWS, own_w), 1)
    own_bias = _bias_from_buckets(jnp.maximum(r_i - c_i, 0), tbl_ref, h)
    near = _bias_from_buckets(bk_ref[...], tbl_ref, h)
    n_key = MOBA_TOPK * MOBA_BLOCK + own_w
    row_sel = lax.broadcasted_iota(jnp.int32, (SAMPLE_ROWS, n_key), 0)
    key_col = lax.broadcasted_iota(jnp.int32, (1, n_key), 1)
    out = jnp.zeros((SAMPLE_ROWS, MOBA_HD), jnp.float32)
    out_row = lax.broadcasted_iota(jnp.int32, out.shape, 0)
    for t in range(t_real):
        keys = jnp.concatenate([ks_ref[t].astype(bf16), kn], axis=0)
        vals = jnp.concatenate([vs_ref[t].astype(bf16), vn], axis=0)
        bias = []
        for r in range(MOBA_TOPK):
            blk = idx_ref[((db * MOBA_HEADS + h) * SAMPLE_ROWS + t) * MOBA_TOPK + r]
            bias.append(jnp.where(blk == last_blk, near[t:t + 1, :], far_bias))
        bias.append(own_bias[t:t + 1, :])
        s_all = lax.dot_general(q, keys, nt, preferred_element_type=jnp.float32)
        s_t = jnp.sum(jnp.where(row_sel == t, s_all, 0.0), axis=0, keepdims=True)
        own_col = key_col - MOBA_TOPK * MOBA_BLOCK
        visible = (own_col < 0) | ((own_col <= t) & (own_col < t_real))
        s_t = jnp.where(visible, s_t + jnp.concatenate(bias, axis=1), NEG_INF)
        m = jnp.max(s_t, axis=1, keepdims=True)
        p = jnp.exp(s_t - m)
        den = jnp.sum(p, axis=1, keepdims=True)
        p8 = jnp.broadcast_to(p, (SAMPLE_ROWS, n_key)).astype(bf16)
        o_t = jnp.dot(p8, vals, preferred_element_type=jnp.float32) / den
        out = jnp.where(out_row == t, o_t, out)
    o_ref[...] = out.astype(o_ref.dtype)


def _sample_attn(page_table, sel_idx, rel_tbl, q_arr, q_blk0, kv, cache_k, cache_v, layer,
                 t_real, out_dtype):
    ndb, n_pages = page_table.shape
    n = kv.shape[0]
    past = n_pages * PAGE_SIZE
    last0 = past - MOBA_BLOCK
    t = np.arange(SAMPLE_ROWS)
    r = np.arange(MOBA_BLOCK)
    near_buckets = jnp.asarray(_rel_bucket_np(past + t[:, None] - (last0 + r[None, :])))
    grid_spec = pltpu.PrefetchScalarGridSpec(
        num_scalar_prefetch=3,
        grid=(ndb, MOBA_HEADS),
        in_specs=[
            pl.BlockSpec((SAMPLE_ROWS, MOBA_HD), lambda d, h, *_: (d, q_blk0 + h)),
            pl.BlockSpec((SAMPLE_ROWS, MOBA_HD), lambda d, h, *_: (d, h)),
            pl.BlockSpec((SAMPLE_ROWS, MOBA_HD), lambda d, h, *_: (d, MOBA_HEADS + h)),
            pl.BlockSpec((SAMPLE_ROWS, MOBA_BLOCK), lambda d, h, *_: (0, 0)),
            pl.BlockSpec(memory_space=pl.ANY),
            pl.BlockSpec(memory_space=pl.ANY),
        ],
        out_specs=pl.BlockSpec((SAMPLE_ROWS, MOBA_HD), lambda d, h, *_: (d, h)),
        scratch_shapes=[
            pltpu.VMEM((t_real, MOBA_TOPK * MOBA_BLOCK, MOBA_HD), jnp.float32),
            pltpu.VMEM((t_real, MOBA_TOPK * MOBA_BLOCK, MOBA_HD), jnp.float32),
            pltpu.SemaphoreType.DMA((2,)),
        ],
    )
    return pl.pallas_call(
        functools.partial(_sample_attn_kernel, n_pages, t_real, layer),
        out_shape=jax.ShapeDtypeStruct((n, D_MODEL), out_dtype),
        grid_spec=grid_spec,
        compiler_params=_cparams(("arbitrary", "arbitrary")),
        name="sample_attn",
    )(page_table.reshape(-1), sel_idx, rel_tbl, q_arr, kv, kv, near_buckets, cache_k, cache_v)


def _layer_norm(x, g, b):
    mu = jnp.mean(x, axis=-1, keepdims=True)
    xc = x - mu
    var = jnp.mean(xc * xc, axis=-1, keepdims=True)
    return xc * lax.rsqrt(var + LN_EPS) * g + b


def _mix_kernel(r_ref, m_ref, ga_ref, gb_ref, x_ref, wr_ref, wm_ref, wo_ref, g_ref, b_ref,
                wh_ref, wl_ref, x1_ref, ri_ref, rg_ref, cnt_ref, carry_ref):
    i = pl.program_id(0)

    @pl.when(i == 0)
    def _():
        carry_ref[...] = jnp.zeros_like(carry_ref)

    f32 = jnp.float32
    a = jnp.dot(r_ref[...].astype(jnp.bfloat16), wr_ref[...], preferred_element_type=f32)
    m = jnp.dot(m_ref[...].astype(jnp.bfloat16), wm_ref[...], preferred_element_type=f32)
    mix = _sigmoid(ga_ref[...].astype(f32)) * a + _sigmoid(gb_ref[...].astype(f32)) * m
    mixed = jnp.dot(mix.astype(jnp.bfloat16), wo_ref[...], preferred_element_type=f32)
    x1 = _layer_norm(DN_ALPHA * x_ref[...] + mixed, g_ref[...], b_ref[...])
    x1_ref[...] = x1

    xh = x1.astype(jnp.bfloat16)
    xl = (x1 - xh.astype(f32)).astype(jnp.bfloat16)
    logits = (jnp.dot(xh, wh_ref[...], preferred_element_type=f32)
              + jnp.dot(xl, wh_ref[...], preferred_element_type=f32)
              + jnp.dot(xh, wl_ref[...], preferred_element_type=f32))
    lane = lax.broadcasted_iota(jnp.int32, logits.shape, 1)
    is_g = lane < MOE_GROUPS
    lgm = jnp.where(is_g, logits, -jnp.inf)
    mg = jnp.max(lgm, axis=1, keepdims=True)
    grp = jnp.min(jnp.where(lgm == mg, lane, LANES), axis=1, keepdims=True)
    pg = 1.0 / jnp.sum(jnp.where(is_g, jnp.exp(logits - mg), 0.0), axis=1, keepdims=True)
    e_lane = lane - MOE_GROUPS
    in_grp = (e_lane >= 0) & (e_lane < MOE_EXPERTS) & ((e_lane >> 3) == grp)
    v1 = jnp.where(in_grp, logits, -jnp.inf)
    t1 = jnp.max(v1, axis=1, keepdims=True)
    i1 = jnp.min(jnp.where(v1 == t1, lane, LANES), axis=1, keepdims=True)
    v2 = jnp.where(lane == i1, -jnp.inf, v1)
    t2 = jnp.max(v2, axis=1, keepdims=True)
    i2 = jnp.min(jnp.where(v2 == t2, lane, LANES), axis=1, keepdims=True)
    z = jnp.exp(t2 - t1)
    g1 = pg / (1.0 + z)
    g2 = pg * z / (1.0 + z)
    oh = jnp.where((lane == i1) | (lane == i2), 1.0, 0.0)
    tm = oh.shape[0]
    tri = jnp.where(lax.broadcasted_iota(jnp.int32, (tm, tm), 0)
                    > lax.broadcasted_iota(jnp.int32, (tm, tm), 1), 1.0, 0.0)
    cum = jnp.dot(tri.astype(jnp.bfloat16), oh.astype(jnp.bfloat16),
                  preferred_element_type=f32) + carry_ref[...]
    r1 = jnp.sum(jnp.where(lane == i1, cum, 0.0), axis=1, keepdims=True)
    r2 = jnp.sum(jnp.where(lane == i2, cum, 0.0), axis=1, keepdims=True)
    carry_ref[...] = carry_ref[...] + jnp.sum(oh, axis=0, keepdims=True)
    ri = jnp.where(lane == 0, i1 - MOE_GROUPS,
                   jnp.where(lane == 1, i2 - MOE_GROUPS,
                             jnp.where(lane == 2, r1.astype(jnp.int32),
                                       jnp.where(lane == 3, r2.astype(jnp.int32), 0))))
    ri_ref[...] = ri
    rg_ref[...] = jnp.where(lane == 0, g1, jnp.where(lane == 1, g2, 0.0))
    cnt_ref[...] = carry_ref[...]


def _mix(ret_act, moba_o, gates, x, w_ret_o, w_moba_o, w_out, ln_g, ln_b, w_hi, w_lo, tm):
    n = x.shape[0]
    full = lambda shape: pl.BlockSpec(shape, lambda i: (0, 0))
    return pl.pallas_call(
        _mix_kernel,
        out_shape=(jax.ShapeDtypeStruct((n, D_MODEL), jnp.float32),
                   jax.ShapeDtypeStruct((n, LANES), jnp.int32),
                   jax.ShapeDtypeStruct((n, LANES), jnp.float32),
                   jax.ShapeDtypeStruct((1, LANES), jnp.float32)),
        grid=(n // tm,),
        in_specs=[
            pl.BlockSpec((tm, RET_W), lambda i: (i, 0)),
            pl.BlockSpec((tm, D_MODEL), lambda i: (i, 0)),
            pl.BlockSpec((tm, D_MODEL), lambda i: (i, 0)),
            pl.BlockSpec((tm, D_MODEL), lambda i: (i, 1)),
            pl.BlockSpec((tm, D_MODEL), lambda i: (i, 0)),
            full((RET_W, D_MODEL)), full((D_MODEL, D_MODEL)), full((D_MODEL, D_MODEL)),
            full((1, D_MODEL)), full((1, D_MODEL)),
            full((D_MODEL, LANES)), full((D_MODEL, LANES)),
        ],
        out_specs=(pl.BlockSpec((tm, D_MODEL), lambda i: (i, 0)),
                   pl.BlockSpec((tm, LANES), lambda i: (i, 0)),
                   pl.BlockSpec((tm, LANES), lambda i: (i, 0)),
                   pl.BlockSpec((1, LANES), lambda i: (0, 0))),
        scratch_shapes=[pltpu.VMEM((1, LANES), jnp.float32)],
        compiler_params=_cparams(("arbitrary",)),
        name="mix_ln_router",
    )(ret_act, moba_o, gates, gates, x, w_ret_o, w_moba_o, w_out, ln_g, ln_b, w_hi, w_lo)


def _load_dest(dest_hbm, dest_ref, sem):
    cp = pltpu.make_async_copy(dest_hbm.at[pl.program_id(0)], dest_ref, sem)
    cp.start()
    cp.wait()


def _dest_at(dest_ref, t, k):
    e = 2 * t + k
    return dest_ref[lax.shift_right_logical(e, 7), lax.bitwise_and(e, LANES - 1)]


def _dispatch_kernel(tm, dest_hbm, x_ref, z_hbm, o_hbm, dest_ref, sem, isem):
    del z_hbm
    _load_dest(dest_hbm, dest_ref, isem)

    def row_copy(t, k):
        return pltpu.make_async_copy(x_ref.at[pl.ds(t, 1)],
                                     o_hbm.at[pl.ds(_dest_at(dest_ref, t, k), 1)], sem)

    def start(t, carry):
        row_copy(t, 0).start()
        row_copy(t, 1).start()
        return carry

    def wait(t, carry):
        row_copy(t, 0).wait()
        row_copy(t, 1).wait()
        return carry

    lax.fori_loop(0, tm, start, 0)
    lax.fori_loop(0, tm, wait, 0)


def _dispatch(x1, dest, n_rows, tm):
    n = x1.shape[0]
    return pl.pallas_call(
        functools.partial(_dispatch_kernel, tm),
        out_shape=jax.ShapeDtypeStruct((n_rows, D_MODEL), jnp.float32),
        grid=(n // tm,),
        in_specs=[pl.BlockSpec(memory_space=pl.ANY),
                  pl.BlockSpec((tm, D_MODEL), lambda i: (i, 0)),
                  pl.BlockSpec(memory_space=pl.ANY)],
        out_specs=pl.BlockSpec(memory_space=pl.ANY),
        scratch_shapes=[pltpu.SMEM((2 * tm // LANES, LANES), jnp.int32),
                        pltpu.SemaphoreType.DMA, pltpu.SemaphoreType.DMA],
        input_output_aliases={2: 0},
        compiler_params=_cparams(("arbitrary",)),
        name="moe_dispatch",
    )(dest, x1, jnp.zeros((n_rows, D_MODEL), jnp.float32))


def _experts_kernel(be_ref, nu_ref, x_ref, wg_ref, wu_ref, wd_ref, o_ref, wgb, wub, wdb):
    i = pl.program_id(0)
    used = i < nu_ref[0]
    first = (i == 0) | (be_ref[i] != be_ref[jnp.maximum(i - 1, 0)])

    @pl.when(used & first)
    def _():
        wgb[...] = wg_ref[...].astype(jnp.bfloat16)
        wub[...] = wu_ref[...].astype(jnp.bfloat16)
        wdb[...] = wd_ref[...].astype(jnp.bfloat16)

    @pl.when(used)
    def _():
        xb = x_ref[...].astype(jnp.bfloat16)
        g = jnp.dot(xb, wgb[...], preferred_element_type=jnp.float32)
        u = jnp.dot(xb, wub[...], preferred_element_type=jnp.float32)
        hid = (g * _sigmoid(g) * u).astype(jnp.bfloat16)
        o_ref[...] = jnp.dot(hid, wdb[...], preferred_element_type=jnp.float32)

    @pl.when(jnp.logical_not(used))
    def _():
        o_ref[...] = jnp.zeros_like(o_ref)


def _experts(xs, blk_exp, n_used, w_gate, w_up, w_down, layer):
    n_rows = xs.shape[0]
    r = MOE_ROWS
    grid_spec = pltpu.PrefetchScalarGridSpec(
        num_scalar_prefetch=2,
        grid=(n_rows // r,),
        in_specs=[
            pl.BlockSpec((r, D_MODEL), lambda i, be, nu: (i, 0)),
            pl.BlockSpec((None, None, D_MODEL, MOE_FF), lambda i, be, nu: (layer, be[i], 0, 0)),
            pl.BlockSpec((None, None, D_MODEL, MOE_FF), lambda i, be, nu: (layer, be[i], 0, 0)),
            pl.BlockSpec((None, None, MOE_FF, D_MODEL), lambda i, be, nu: (layer, be[i], 0, 0)),
        ],
        out_specs=pl.BlockSpec((r, D_MODEL), lambda i, be, nu: (i, 0)),
        scratch_shapes=[pltpu.VMEM((D_MODEL, MOE_FF), jnp.bfloat16),
                        pltpu.VMEM((D_MODEL, MOE_FF), jnp.bfloat16),
                        pltpu.VMEM((MOE_FF, D_MODEL), jnp.bfloat16)],
    )
    return pl.pallas_call(
        _experts_kernel,
        out_shape=jax.ShapeDtypeStruct((n_rows, D_MODEL), jnp.float32),
        grid_spec=grid_spec,
        compiler_params=_cparams(("arbitrary",)),
        name="moe_experts",
    )(blk_exp, n_used, xs, w_gate, w_up, w_down)


def _combine_kernel(tm, dest_hbm, x_ref, gt_ref, g_ref, b_ref, y_hbm, o_ref, ob_ref,
                    ya_ref, yb_ref, dest_ref, sem, isem):
    _load_dest(dest_hbm, dest_ref, isem)

    def row_copy(t, k):
        buf = ya_ref if k == 0 else yb_ref
        return pltpu.make_async_copy(y_hbm.at[pl.ds(_dest_at(dest_ref, t, k), 1)],
                                     buf.at[pl.ds(t, 1)], sem.at[k])

    def start(t, carry):
        row_copy(t, 0).start()
        row_copy(t, 1).start()
        return carry

    def wait(t, carry):
        row_copy(t, 0).wait()
        row_copy(t, 1).wait()
        return carry

    lax.fori_loop(0, tm, start, 0)
    lax.fori_loop(0, tm, wait, 0)
    gt = gt_ref[...]
    y = gt[:, 0:1] * ya_ref[...] + gt[:, 1:2] * yb_ref[...]
    x2 = _layer_norm(DN_ALPHA * x_ref[...] + y, g_ref[...], b_ref[...])
    o_ref[...] = x2
    ob_ref[...] = x2.astype(jnp.bfloat16)


def _combine(x1, gates, dest, ys, ln_g, ln_b, tm):
    n = x1.shape[0]
    return pl.pallas_call(
        functools.partial(_combine_kernel, tm),
        out_shape=(jax.ShapeDtypeStruct((n, D_MODEL), jnp.float32),
                   jax.ShapeDtypeStruct((n, D_MODEL), jnp.bfloat16)),
        grid=(n // tm,),
        in_specs=[pl.BlockSpec(memory_space=pl.ANY),
                  pl.BlockSpec((tm, D_MODEL), lambda i: (i, 0)),
                  pl.BlockSpec((tm, LANES), lambda i: (i, 0)),
                  pl.BlockSpec((1, D_MODEL), lambda i: (0, 0)),
                  pl.BlockSpec((1, D_MODEL), lambda i: (0, 0)),
                  pl.BlockSpec(memory_space=pl.ANY)],
        out_specs=(pl.BlockSpec((tm, D_MODEL), lambda i: (i, 0)),
                   pl.BlockSpec((tm, D_MODEL), lambda i: (i, 0))),
        scratch_shapes=[pltpu.VMEM((tm, D_MODEL), jnp.float32),
                        pltpu.VMEM((tm, D_MODEL), jnp.float32),
                        pltpu.SMEM((2 * tm // LANES, LANES), jnp.int32),
                        pltpu.SemaphoreType.DMA((2,)), pltpu.SemaphoreType.DMA],
        compiler_params=_cparams(("arbitrary",)),
        name="moe_combine_ln",
    )(dest, x1, gates, ln_g, ln_b, ys)


def _moe(x1, route_i, route_g, counts, w_gate, w_up, w_down, layer, ln_g, ln_b, tm):
    n = x1.shape[0]
    r = MOE_ROWS
    n_blk = (2 * n) // r + MOE_EXPERTS
    cnt = counts[0, MOE_GROUPS:MOE_GROUPS + MOE_EXPERTS].astype(jnp.int32)
    padded = (cnt + r - 1) // r * r
    pad_end = jnp.cumsum(padded)
    pad_start = pad_end - padded
    dest = (pad_start[route_i[:, 0:2]] + route_i[:, 2:4]).astype(jnp.int32)
    dest = dest.reshape(n // tm, 2 * tm // LANES, LANES)
    blk_exp = jnp.minimum(
        jnp.searchsorted(pad_end, jnp.arange(n_blk, dtype=jnp.int32) * r, side="right"),
        MOE_EXPERTS - 1).astype(jnp.int32)
    n_used = (pad_end[-1:] // r).astype(jnp.int32)
    xs = _dispatch(x1, dest, n_blk * r, tm)
    ys = _experts(xs, blk_exp, n_used, w_gate, w_up, w_down, layer)
    return _combine(x1, route_g, dest, ys, ln_g, ln_b, tm)


def _split_hi_lo(w):
    hi = w.astype(jnp.bfloat16)
    return hi, (w - hi.astype(jnp.float32)).astype(jnp.bfloat16)


def _layer_common(x, xb, attn_fn, lw, layer, tm_proj, tm, act_dtype):
    (w_in, col_scale, w_ret_o, w_moba_o, w_out, ln1_g, ln1_b, w_hi, w_lo,
     w_gate, w_up, w_down, ln2_g, ln2_b) = lw
    proj = functools.partial(_proj, xb, w_in, layer, col_scale)
    qk = proj(COL_RQ, 2, jnp.float32, tm_proj)
    vgq = proj(COL_RV, 5, act_dtype, tm_proj)
    kv = proj(COL_MK, 2, jnp.float32, tm_proj)
    gates = proj(COL_GA, 2, act_dtype, tm_proj)
    ret_act, ret_state, moba_o = attn_fn(qk, vgq, kv)
    x1, route_i, route_g, counts = _mix(ret_act, moba_o, gates, x, w_ret_o, w_moba_o, w_out,
                                        ln1_g, ln1_b, w_hi, w_lo, tm)
    x2, x2b = _moe(x1, route_i, route_g, counts, w_gate, w_up, w_down, layer, ln2_g, ln2_b, tm)
    return x2, x2b, kv, ret_state


def kernel(x_prompt, x_sample, cache_k, cache_v, state_ret, page_table, rel_bias, w_in, ret_gn_w,
           w_ret_o, w_moba_o, w_out, ln1_g, ln1_b, w_group, w_router, w_exp_gate, w_exp_up,
           w_exp_down, ln2_g, ln2_b):
    b, s, _ = x_prompt.shape
    db, t_real, _ = x_sample.shape
    n_pages = page_table.shape[1]
    past = n_pages * PAGE_SIZE
    bf16 = jnp.bfloat16

    col_scale = np.ones((1, PROJ_TILES * D_MODEL), np.float32)
    col_scale[0, D_MODEL:2 * D_MODEL] = RET_DK ** -0.5
    col_scale[0, COL_MQ * D_MODEL:(COL_MQ + 1) * D_MODEL] = MOBA_HD ** -0.5
    col_scale = jnp.asarray(col_scale)
    rel_tbl = rel_bias.T.reshape(-1)

    cos_p, sin_p = _xpos_tables(jnp.arange(s))
    pos_s = jnp.minimum(past + jnp.arange(SAMPLE_ROWS), past + t_real - 1)
    cos_s, sin_s = _xpos_tables(pos_s)

    xp = x_prompt.reshape(b * s, D_MODEL)
    xs = jnp.pad(x_sample, ((0, 0), (0, SAMPLE_ROWS - t_real), (0, 0))).reshape(
        db * SAMPLE_ROWS, D_MODEL)
    xpb, xsb = xp.astype(bf16), xs.astype(bf16)
    psums = _page_sums(cache_k)
    zero_state = jnp.zeros((b, RET_HEADS, RET_DK, RET_DV), jnp.float32)
    state_all = state_ret.reshape(DEPTH * db, RET_HEADS, RET_DK, RET_DV)
    mq_blk = (COL_MQ - COL_RV) * D_MODEL // MOBA_HD

    kp, vp, rp, kss, vss, rss = [], [], [], [], [], []
    for l in range(DEPTH):
        w_gr = jnp.zeros((D_MODEL, LANES), jnp.float32)
        w_gr = w_gr.at[:, :MOE_GROUPS].set(w_group[l])
        w_gr = w_gr.at[:, MOE_GROUPS:MOE_GROUPS + MOE_EXPERTS].set(w_router[l])
        w_hi, w_lo = _split_hi_lo(w_gr)
        lw = (w_in, col_scale, w_ret_o[l].astype(bf16), w_moba_o[l].astype(bf16),
              w_out[l].astype(bf16), ln1_g[l][None], ln1_b[l][None], w_hi, w_lo,
              w_exp_gate, w_exp_up, w_exp_down, ln2_g[l][None], ln2_b[l][None])
        gn = ret_gn_w[l][None]

        def prompt_attn(qk, vgq, kv):
            ret_act, st = _retention(qk, vgq, 0, 1, cos_p, sin_p, gn, zero_state, 0,
                                     b, s // RET_CHUNK, RET_CHUNK, RET_CHUNK, bf16)
            moba_o = _moba_prompt(vgq, mq_blk, kv, rel_tbl, b, s, bf16)
            return ret_act, st, moba_o

        def sample_attn(qk, vgq, kv, l=l):
            ret_act, st = _retention(qk, vgq, 0, 1, cos_s, sin_s, gn, state_all, l * db,
                                     db, 1, SAMPLE_ROWS, t_real, jnp.float32)
            sel = _sample_select(page_table, vgq, COL_MQ - COL_RV, psums, l)
            sel_idx = sel[:, :, :, :MOBA_TOPK].reshape(-1)
            moba_o = _sample_attn(page_table, sel_idx, rel_tbl, vgq, mq_blk, kv,
                                  cache_k, cache_v, l, t_real, jnp.float32)
            return ret_act, st, moba_o

        n_s = db * SAMPLE_ROWS
        xp, xpb, kv_p, st_p = _layer_common(xp, xpb, prompt_attn, lw, l, 512, 256, bf16)
        xs, xsb, kv_s, st_s = _layer_common(xs, xsb, sample_attn, lw, l, min(256, n_s),
                                            min(256, n_s), jnp.float32)
        kp.append(kv_p[:, :D_MODEL].reshape(b, s // PAGE_SIZE, PAGE_SIZE, MOBA_HEADS, MOBA_HD))
        vp.append(kv_p[:, D_MODEL:].reshape(b, s // PAGE_SIZE, PAGE_SIZE, MOBA_HEADS, MOBA_HD))
        rp.append(st_p)
        kv_s = kv_s.reshape(db, SAMPLE_ROWS, 2, MOBA_HEADS, MOBA_HD)[:, :t_real]
        kss.append(kv_s[:, :, 0])
        vss.append(kv_s[:, :, 1])
        rss.append(st_s)

    yp = xp.reshape(b, s, D_MODEL)
    ys = xs.reshape(db, SAMPLE_ROWS, D_MODEL)[:, :t_real]
    return (yp, ys, jnp.stack(kp), jnp.stack(vp), jnp.stack(rp), jnp.stack(kss), jnp.stack(vss),
            jnp.stack(rss))
```

```python
import functools
import math

import numpy as np
import jax
import jax.numpy as jnp
from jax import lax
from jax.experimental import pallas as pl
from jax.experimental.pallas import tpu as pltpu

D_MODEL = 1024
DEPTH = 2
PAGE_SIZE = 128
RET_HEADS = 4
RET_DK = D_MODEL // RET_HEADS
RET_DV = 2 * D_MODEL // RET_HEADS
RET_W = RET_HEADS * RET_DV
RET_CHUNK = 128
XPOS_BASE = 10000.0
GN_EPS = 1e-6
MOBA_HEADS = 8
MOBA_HD = D_MODEL // MOBA_HEADS
MOBA_BLOCK = 256
MOBA_TOPK = 3
REL_BUCKETS = 32
REL_MAX_DIST = 128
MOE_GROUPS = 4
MOE_EPG = 8
MOE_EXPERTS = MOE_GROUPS * MOE_EPG
MOE_FF = D_MODEL // 2
DN_ALPHA = (2.0 * DEPTH) ** 0.25
LN_EPS = 1e-5
NEG_INF = -1e30
LOG2E = math.log2(math.e)

COL_RQ, COL_RV, COL_RG, COL_MQ, COL_MK, COL_GA = 0, 2, 4, 6, 7, 9
PROJ_TILES = 11

LANES = 128
SUBLANES = 8
VMEM_LIMIT = 56 * 1024 * 1024
SAMPLE_ROWS = 8
MOE_ROWS = 256
PAGES_PER_STEP = 8
ROW_DMA_UNROLL = 8
MOBA_GROUP = 4


def _cparams(sem):
    return pltpu.CompilerParams(dimension_semantics=sem, vmem_limit_bytes=VMEM_LIMIT)


def _sigmoid(x):
    return 1.0 / (1.0 + jnp.exp(-x))


def _proj_kernel(x_ref, w_ref, s_ref, o_ref, wb_ref):
    @pl.when(pl.program_id(1) == 0)
    def _():
        wb_ref[...] = w_ref[...].astype(jnp.bfloat16)

    acc = jnp.dot(x_ref[...], wb_ref[...], preferred_element_type=jnp.float32)
    o_ref[...] = (acc * s_ref[...]).astype(o_ref.dtype)


def _proj(xb, w_in, layer, col_scale, col0, ncols, out_dtype, tm):
    n = xb.shape[0]
    tn = D_MODEL
    return pl.pallas_call(
        _proj_kernel,
        out_shape=jax.ShapeDtypeStruct((n, ncols * tn), out_dtype),
        grid=(ncols, n // tm),
        in_specs=[
            pl.BlockSpec((tm, D_MODEL), lambda j, i: (i, 0)),
            pl.BlockSpec((None, D_MODEL, tn), lambda j, i: (layer, 0, j + col0)),
            pl.BlockSpec((1, tn), lambda j, i: (0, j + col0)),
        ],
        out_specs=pl.BlockSpec((tm, tn), lambda j, i: (i, j)),
        scratch_shapes=[pltpu.VMEM((D_MODEL, tn), jnp.bfloat16)],
        compiler_params=_cparams(("arbitrary", "arbitrary")),
        name="in_proj",
    )(xb, w_in, col_scale)


def _proj_layer_kernel(x_ref, w_ref, s_ref, prev_ref, o_ref, wb_ref):
    del prev_ref
    _proj_kernel(x_ref, w_ref, s_ref, o_ref, wb_ref)


def _proj_into(xb, w_in, layer, col_scale, col, prev, tm):
    n = xb.shape[0]
    tn = D_MODEL
    return pl.pallas_call(
        _proj_layer_kernel,
        out_shape=jax.ShapeDtypeStruct((DEPTH, n, tn), jnp.float32),
        grid=(1, n // tm),
        in_specs=[
            pl.BlockSpec((tm, D_MODEL), lambda j, i: (i, 0)),
            pl.BlockSpec((None, D_MODEL, tn), lambda j, i: (layer, 0, col)),
            pl.BlockSpec((1, tn), lambda j, i: (0, col)),
            pl.BlockSpec(memory_space=pl.ANY),
        ],
        out_specs=pl.BlockSpec((None, tm, tn), lambda j, i: (layer, i, 0)),
        scratch_shapes=[pltpu.VMEM((D_MODEL, tn), jnp.bfloat16)],
        input_output_aliases={3: 0},
        compiler_params=_cparams(("arbitrary", "arbitrary")),
        name="in_proj_kv",
    )(xb, w_in, col_scale, prev)


def _ret_consts(c, c_true):
    lg = np.log1p(-np.exp2(-5.0 - np.arange(RET_HEADS, dtype=np.float64)))
    j = np.arange(c, dtype=np.float64)
    diff = j[:, None] - j[None, :]
    dmask = np.where(diff >= 0, np.exp(lg[:, None, None] * np.maximum(diff, 0.0)), 0.0)
    qdec = np.exp(lg[:, None] * (j + 1.0))[..., None]
    kdec = np.where(j < c_true, np.exp(lg[:, None] * (c_true - 1.0 - j)), 0.0)[..., None]
    sdec = [float(v) for v in np.exp(lg * c_true)]
    return (jnp.asarray(dmask, jnp.float32), jnp.asarray(qdec, jnp.float32),
            jnp.asarray(kdec, jnp.float32), sdec)


def _ret_kernel(sdec, q_ref, k_ref, v_ref, g_ref, cos_ref, sin_ref, dm_ref, qd_ref, kd_ref,
                gn_ref, s0_ref, o_ref, sf_ref, st_ref):
    c = pl.program_id(1)

    @pl.when(c == 0)
    def _():
        st_ref[...] = s0_ref[0]

    cos = cos_ref[...]
    sin = sin_ref[...]
    half = RET_DK // 2

    def rot(x):
        x1, x2 = x[:, :half], x[:, half:]
        return jnp.concatenate([x1 * cos - x2 * sin, x1 * sin + x2 * cos], axis=-1)

    for h in range(RET_HEADS):
        qr = rot(q_ref[:, h * RET_DK:(h + 1) * RET_DK])
        kr = rot(k_ref[:, h * RET_DK:(h + 1) * RET_DK])
        qb = qr.astype(jnp.bfloat16)
        kb = kr.astype(jnp.bfloat16)
        v = v_ref[:, h * RET_DV:(h + 1) * RET_DV].astype(jnp.bfloat16)
        st = st_ref[h]
        s = lax.dot_general(qb, kb, (((1,), (1,)), ((), ())),
                            preferred_element_type=jnp.float32) * dm_ref[h]
        inner = jnp.dot(s.astype(jnp.bfloat16), v, preferred_element_type=jnp.float32)
        cross = jnp.dot(qb, st.astype(jnp.bfloat16),
                        preferred_element_type=jnp.float32) * qd_ref[h]
        kdb = (kr * kd_ref[h]).astype(jnp.bfloat16)
        st_ref[h] = sdec[h] * st + lax.dot_general(
            kdb, v, (((0,), (0,)), ((), ())), preferred_element_type=jnp.float32)
        o = inner + cross
        mu = jnp.mean(o, axis=-1, keepdims=True)
        oc = o - mu
        var = jnp.mean(oc * oc, axis=-1, keepdims=True)
        on = oc * lax.rsqrt(var + GN_EPS) * gn_ref[:, h * RET_DV:(h + 1) * RET_DV]
        g = g_ref[:, h * RET_DV:(h + 1) * RET_DV].astype(jnp.float32)
        o_ref[:, h * RET_DV:(h + 1) * RET_DV] = (on * (g * _sigmoid(g))).astype(o_ref.dtype)

    @pl.when(c == pl.num_programs(1) - 1)
    def _():
        sf_ref[0] = st_ref[...]


def _ret_layer_kernel(sdec, *refs):
    _ret_kernel(sdec, *refs[:11], *refs[12:])


def _retention(qk, vg, v_blk, g_blk, cos, sin, gn_w, state0, s0_off, nb, nc, c, c_true,
               out_dtype, sf_rows=None, sf_off=0, sf_prev=None):
    n = qk.shape[0]
    sf_rows = nb if sf_rows is None else sf_rows
    dmask, qdec, kdec, sdec = _ret_consts(c, c_true)
    row = lambda b, i: b * nc + i
    extra_specs = [] if sf_prev is None else [pl.BlockSpec(memory_space=pl.ANY)]
    extra_args = [] if sf_prev is None else [sf_prev]
    return pl.pallas_call(
        functools.partial(_ret_kernel if sf_prev is None else _ret_layer_kernel, sdec),
        out_shape=(jax.ShapeDtypeStruct((n, RET_W), out_dtype),
                   jax.ShapeDtypeStruct((sf_rows, RET_HEADS, RET_DK, RET_DV), jnp.float32)),
        grid=(nb, nc),
        input_output_aliases={} if sf_prev is None else {11: 1},
        in_specs=[
            pl.BlockSpec((c, D_MODEL), lambda b, i: (row(b, i), 0)),
            pl.BlockSpec((c, D_MODEL), lambda b, i: (row(b, i), 1)),
            pl.BlockSpec((c, RET_W), lambda b, i: (row(b, i), v_blk)),
            pl.BlockSpec((c, RET_W), lambda b, i: (row(b, i), g_blk)),
            pl.BlockSpec((c, RET_DK // 2), lambda b, i: (i, 0)),
            pl.BlockSpec((c, RET_DK // 2), lambda b, i: (i, 0)),
            pl.BlockSpec((RET_HEADS, c, c), lambda b, i: (0, 0, 0)),
            pl.BlockSpec((RET_HEADS, c, 1), lambda b, i: (0, 0, 0)),
            pl.BlockSpec((RET_HEADS, c, 1), lambda b, i: (0, 0, 0)),
            pl.BlockSpec((1, RET_W), lambda b, i: (0, 0)),
            pl.BlockSpec((1, RET_HEADS, RET_DK, RET_DV), lambda b, i: (s0_off + b, 0, 0, 0)),
        ] + extra_specs,
        out_specs=(pl.BlockSpec((c, RET_W), lambda b, i: (row(b, i), 0)),
                   pl.BlockSpec((1, RET_HEADS, RET_DK, RET_DV),
                                lambda b, i: (sf_off + b, 0, 0, 0))),
        scratch_shapes=[pltpu.VMEM((RET_HEADS, RET_DK, RET_DV), jnp.float32)],
        compiler_params=_cparams(("arbitrary", "arbitrary")),
        name="retention",
    )(qk, qk, vg, vg, cos, sin, dmask, qdec, kdec, gn_w, state0, *extra_args)


def _xpos_tables(pos):
    half = RET_DK // 2
    inv = 1.0 / (XPOS_BASE ** jnp.linspace(0.0, 1.0, half, dtype=jnp.float32))
    ang = pos.astype(jnp.float32)[:, None] * inv[None, :]
    return jnp.cos(ang), jnp.sin(ang)


def _rel_bucket_np(rel):
    n = np.maximum(rel, 0)
    exact = REL_BUCKETS // 2
    nf = np.maximum(n, 1).astype(np.float32)
    large = exact + (np.log(nf / np.float32(exact)) / np.float32(math.log(REL_MAX_DIST / exact))
                     * np.float32(REL_BUCKETS - exact)).astype(np.int32)
    return np.where(n < exact, n, np.minimum(large, REL_BUCKETS - 1)).astype(np.int32)


def _bias_from_buckets(bucket, table_ref, h):
    out = jnp.zeros(bucket.shape, jnp.float32)
    for b in range(REL_BUCKETS):
        out = jnp.where(bucket == b, table_ref[h * REL_BUCKETS + b], out)
    return out


def _top3_mask(scores, valid, row, nrow):
    sc = jnp.where(valid, scores, NEG_INF)
    sel = jnp.zeros(scores.shape, jnp.float32)
    for _ in range(MOBA_TOPK):
        m = jnp.max(sc, axis=0, keepdims=True)
        idx = jnp.min(jnp.where(sc == m, row, nrow), axis=0, keepdims=True)
        pick = row == idx
        sel = jnp.where(pick, 1.0, sel)
        sc = jnp.where(pick, -jnp.inf, sc)
    return jnp.where(valid, sel, 0.0)


def _moba_prompt_kernel(nblk, grp, tbl_ref, q_ref, k_ref, v_ref, bk_ref, o_ref,
                        kb_ref, vt_ref, mh_ref, ml_ref, bown_ref, bprev_ref, sel_ref, acc_ref,
                        sa_ref, sb_ref):
    h = pl.program_id(1)
    qi = pl.program_id(2)
    blk = MOBA_BLOCK
    bf16 = jnp.bfloat16
    f32 = jnp.float32
    nt = (((1,), (1,)), ((), ()))

    @pl.when(qi == 0)
    def _():
        kf = k_ref[...]
        kb_ref[...] = kf.astype(bf16)
        means = jnp.mean(kf.reshape(nblk, blk, MOBA_HD), axis=1)
        mh = means.astype(bf16)
        mh_ref[...] = mh
        ml_ref[...] = (means - mh.astype(f32)).astype(bf16)

        def transpose_block(j, carry):
            r0 = pl.multiple_of(j * blk, blk)
            vt_ref[j] = v_ref[pl.ds(r0, blk), :].T.astype(bf16)
            return carry

        lax.fori_loop(0, nblk, transpose_block, 0)
        bown_ref[...] = _bias_from_buckets(bk_ref[0], tbl_ref, h)
        bprev_ref[...] = _bias_from_buckets(bk_ref[1], tbl_ref, h)

    q = q_ref[...]
    tq = q.shape[0]
    scores = (lax.dot_general(mh_ref[...], q, nt, preferred_element_type=f32)
              + lax.dot_general(ml_ref[...], q, nt, preferred_element_type=f32))
    row = lax.broadcasted_iota(jnp.int32, scores.shape, 0)
    sel_ref[...] = _top3_mask(scores, row < qi, row, nblk)

    def softmax_step(tiles, m_old, l_old):
        m_new = m_old
        for s, _ in tiles:
            m_new = jnp.maximum(m_new, jnp.max(s, axis=0, keepdims=True))
        a = jnp.exp2(m_old - m_new)
        l_new = a * l_old
        upd = None
        for s, vt in tiles:
            p = jnp.exp2(s - m_new)
            l_new = l_new + jnp.sum(p, axis=0, keepdims=True)
            d = jnp.dot(vt, p.astype(bf16), preferred_element_type=f32)
            upd = d if upd is None else upd + d
        acc_ref[...] = a * acc_ref[...] + upd
        return m_new, l_new

    def picked(n, ok):
        return jnp.where(ok, sel_ref[pl.ds(n, 1), :], 0.0) > 0.0

    def group_base(gi):
        return jnp.minimum(gi * grp, nblk - grp)

    def group_logits(gi):
        r0 = pl.multiple_of(group_base(gi) * blk, blk)
        return lax.dot_general(kb_ref[pl.ds(r0, grp * blk), :], q, nt, preferred_element_type=f32)

    sa_ref[...] = group_logits(0)

    acc_ref[...] = jnp.zeros_like(acc_ref)
    prev = jnp.maximum(qi - 1, 0)
    s_prev = lax.dot_general(kb_ref[pl.ds(pl.multiple_of(prev * blk, blk), blk), :], q, nt,
                             preferred_element_type=f32) + bprev_ref[...]
    s_prev = jnp.where(picked(prev, qi >= 1), s_prev, NEG_INF)
    s_own = lax.dot_general(kb_ref[pl.ds(pl.multiple_of(qi * blk, blk), blk), :], q, nt,
                            preferred_element_type=f32) + bown_ref[...]
    k_i = lax.broadcasted_iota(jnp.int32, s_own.shape, 0)
    q_i = lax.broadcasted_iota(jnp.int32, s_own.shape, 1)
    s_own = jnp.where(q_i >= k_i, s_own, NEG_INF)
    m, l = softmax_step([(s_prev, vt_ref[prev]), (s_own, vt_ref[qi])],
                        jnp.full((1, tq), NEG_INF, f32), jnp.zeros((1, tq), f32))

    far_bias = tbl_ref[h * REL_BUCKETS + REL_BUCKETS - 1]
    n_far = jnp.maximum(qi - 1, 0)

    def far_group(gi, s_ref, m_old, l_old):
        first = gi * grp
        base = group_base(gi)
        tiles = []
        for i in range(grp):
            n = base + i
            keep = picked(n, (n >= first) & (n < n_far))
            tiles.append((s_ref[i * blk:(i + 1) * blk, :] + jnp.where(keep, far_bias, NEG_INF),
                          vt_ref[n]))
        return softmax_step(tiles, m_old, l_old)

    def far_pair(j, carry):
        sb_ref[...] = group_logits(2 * j + 1)
        m1, l1 = far_group(2 * j, sa_ref, *carry)
        sa_ref[...] = group_logits(2 * j + 2)
        return far_group(2 * j + 1, sb_ref, m1, l1)

    n_grp = (n_far + grp - 1) // grp
    m, l = lax.fori_loop(0, (n_grp + 1) // 2, far_pair, (m, l))
    o_ref[...] = (acc_ref[...] / l).T.astype(o_ref.dtype)


def _moba_prompt(q_arr, q_blk0, k_all, v_all, layer, rel_tbl, nb, s, out_dtype):
    n = k_all.shape[1]
    blk = MOBA_BLOCK
    nblk = s // blk
    grp = min(MOBA_GROUP, nblk)
    i = np.arange(blk)
    buckets = jnp.asarray(np.stack([_rel_bucket_np(i[None, :] - i[:, None]),
                                    _rel_bucket_np(blk + i[None, :] - i[:, None])]))
    grid_spec = pltpu.PrefetchScalarGridSpec(
        num_scalar_prefetch=1,
        grid=(nb, MOBA_HEADS, nblk),
        in_specs=[
            pl.BlockSpec((blk, MOBA_HD), lambda b, h, qi, t: (b * nblk + qi, q_blk0 + h)),
            pl.BlockSpec((None, s, MOBA_HD), lambda b, h, qi, t: (layer, b, h)),
            pl.BlockSpec((None, s, MOBA_HD), lambda b, h, qi, t: (layer, b, h)),
            pl.BlockSpec((2, blk, blk), lambda b, h, qi, t: (0, 0, 0)),
        ],
        out_specs=pl.BlockSpec((blk, MOBA_HD), lambda b, h, qi, t: (b * nblk + qi, h)),
        scratch_shapes=[
            pltpu.VMEM((s, MOBA_HD), jnp.bfloat16),
            pltpu.VMEM((nblk, MOBA_HD, blk), jnp.bfloat16),
            pltpu.VMEM((nblk, MOBA_HD), jnp.bfloat16),
            pltpu.VMEM((nblk, MOBA_HD), jnp.bfloat16),
            pltpu.VMEM((blk, blk), jnp.float32),
            pltpu.VMEM((blk, blk), jnp.float32),
            pltpu.VMEM((nblk, blk), jnp.float32),
            pltpu.VMEM((MOBA_HD, blk), jnp.float32),
            pltpu.VMEM((grp * blk, blk), jnp.float32),
            pltpu.VMEM((grp * blk, blk), jnp.float32),
        ],
    )
    return pl.pallas_call(
        functools.partial(_moba_prompt_kernel, nblk, grp),
        out_shape=jax.ShapeDtypeStruct((n, D_MODEL), out_dtype),
        grid_spec=grid_spec,
        compiler_params=_cparams(("arbitrary", "arbitrary", "arbitrary")),
        name="moba_prompt",
    )(rel_tbl, q_arr, k_all, v_all, buckets)


def _page_sum_kernel(c_ref, o_ref):
    o_ref[0] = jnp.sum(c_ref[0], axis=1)


def _page_sums(cache_k):
    nl, npg = cache_k.shape[0], cache_k.shape[1]
    pb = PAGES_PER_STEP
    return pl.pallas_call(
        _page_sum_kernel,
        out_shape=jax.ShapeDtypeStruct((nl, npg, MOBA_HEADS, MOBA_HD), jnp.float32),
        grid=(nl, npg // pb),
        in_specs=[pl.BlockSpec((1, pb, PAGE_SIZE, MOBA_HEADS, MOBA_HD),
                               lambda l, i: (l, i, 0, 0, 0))],
        out_specs=pl.BlockSpec((1, pb, MOBA_HEADS, MOBA_HD), lambda l, i: (l, i, 0, 0)),
        compiler_params=_cparams(("arbitrary", "arbitrary")),
        name="page_sums",
    )(cache_k)


def _sample_select_kernel(n_pages, layer, pt_ref, q_ref, ps_hbm, o_ref, ps_ref, sem):
    db = pl.program_id(0)
    nfull = n_pages // 2

    def page_copy(j):
        dst = (j % 2) * nfull + j // 2
        return pltpu.make_async_copy(ps_hbm.at[layer, pt_ref[db * n_pages + j]],
                                     ps_ref.at[dst], sem)

    def start(j, carry):
        page_copy(j).start()
        return carry

    def wait(j, carry):
        page_copy(j).wait()
        return carry

    lax.fori_loop(0, n_pages, start, 0)
    lax.fori_loop(0, n_pages, wait, 0)

    nt = (((1,), (1,)), ((), ()))
    for h in range(MOBA_HEADS):
        means = (ps_ref[pl.ds(0, nfull), h, :] + ps_ref[pl.ds(nfull, nfull), h, :]) * (
            1.0 / MOBA_BLOCK)
        mh = means.astype(jnp.bfloat16)
        ml = (means - mh.astype(jnp.float32)).astype(jnp.bfloat16)
        q = q_ref[:, h * MOBA_HD:(h + 1) * MOBA_HD].astype(jnp.bfloat16)
        scores = (lax.dot_general(q, mh, nt, preferred_element_type=jnp.float32)
                  + lax.dot_general(q, ml, nt, preferred_element_type=jnp.float32))
        col = lax.broadcasted_iota(jnp.int32, scores.shape, 1)
        sc = scores
        out = jnp.zeros((SAMPLE_ROWS, LANES), jnp.int32)
        lane = lax.broadcasted_iota(jnp.int32, out.shape, 1)
        for r in range(MOBA_TOPK):
            m = jnp.max(sc, axis=1, keepdims=True)
            idx = jnp.min(jnp.where(sc == m, col, nfull), axis=1, keepdims=True)
            out = jnp.where(lane == r, idx, out)
            sc = jnp.where(col == idx, -jnp.inf, sc)
        o_ref[0, h] = out


def _sample_select(page_table, q_arr, q_blk0, psums, layer):
    ndb, n_pages = page_table.shape
    grid_spec = pltpu.PrefetchScalarGridSpec(
        num_scalar_prefetch=1,
        grid=(ndb,),
        in_specs=[
            pl.BlockSpec((SAMPLE_ROWS, D_MODEL), lambda d, pt: (d, q_blk0)),
            pl.BlockSpec(memory_space=pl.ANY),
        ],
        out_specs=pl.BlockSpec((1, MOBA_HEADS, SAMPLE_ROWS, LANES), lambda d, pt: (d, 0, 0, 0)),
        scratch_shapes=[pltpu.VMEM((n_pages, MOBA_HEADS, MOBA_HD), jnp.float32),
                        pltpu.SemaphoreType.DMA],
    )
    return pl.pallas_call(
        functools.partial(_sample_select_kernel, n_pages, layer),
        out_shape=jax.ShapeDtypeStruct((ndb, MOBA_HEADS, SAMPLE_ROWS, LANES), jnp.int32),
        grid_spec=grid_spec,
        compiler_params=_cparams(("arbitrary",)),
        name="sample_select",
    )(page_table.reshape(-1), q_arr, psums)


def _sample_attn_kernel(n_pages, t_real, layer, pt_ref, idx_ref, tbl_ref, q_ref, k_ref, v_ref,
                        bk_ref, ck_hbm, cv_hbm, o_ref, ks_ref, vs_ref, sem):
    db = pl.program_id(0)
    h = pl.program_id(1)
    ppb = MOBA_BLOCK // PAGE_SIZE
    npage = MOBA_TOPK * ppb
    step = db * MOBA_HEADS + h
    n_steps = pl.num_programs(0) * MOBA_HEADS
    slot = lax.rem(step, 2)

    def copies(st, buf, t, j):
        d = st // MOBA_HEADS
        hh = lax.rem(st, MOBA_HEADS)
        blk = idx_ref[(st * SAMPLE_ROWS + t) * MOBA_TOPK + j // ppb]
        page = pt_ref[d * n_pages + blk * ppb + j % ppb]
        rows = pl.ds(j * PAGE_SIZE, PAGE_SIZE)
        return (pltpu.make_async_copy(ck_hbm.at[layer, page, :, hh, :], ks_ref.at[buf, t, rows],
                                      sem.at[buf, 0]),
                pltpu.make_async_copy(cv_hbm.at[layer, page, :, hh, :], vs_ref.at[buf, t, rows],
                                      sem.at[buf, 1]))

    def for_all_copies(st, buf, fn):
        for t in range(t_real):
            for j in range(npage):
                ck, cv = copies(st, buf, t, j)
                fn(ck)
                fn(cv)

    @pl.when(step == 0)
    def _():
        for_all_copies(step, slot, lambda c: c.start())

    @pl.when(step + 1 < n_steps)
    def _():
        for_all_copies(step + 1, 1 - slot, lambda c: c.start())

    for_all_copies(step, slot, lambda c: c.wait())

    nt = (((1,), (1,)), ((), ()))
    bf16 = jnp.bfloat16
    own_w = 2 * SAMPLE_ROWS
    q = q_ref[...].astype(bf16)
    pad = jnp.zeros((own_w - SAMPLE_ROWS, MOBA_HD), jnp.float32)
    kn = jnp.concatenate([k_ref[...], pad], axis=0).astype(bf16)
    vn = jnp.concatenate([v_ref[...], pad], axis=0).astype(bf16)
    far_bias = tbl_ref[h * REL_BUCKETS + REL_BUCKETS - 1]
    last_blk = n_pages // ppb - 1
    r_i = lax.broadcasted_iota(jnp.int32, (SAMPLE_ROWS, own_w), 0)
    c_i = lax.broadcasted_iota(jnp.int32, (SAMPLE_ROWS, own_w), 1)
    own_bias = _bias_from_buckets(jnp.maximum(r_i - c_i, 0), tbl_ref, h)
    near = _bias_from_buckets(bk_ref[...], tbl_ref, h)
    n_key = MOBA_TOPK * MOBA_BLOCK + own_w
    row_sel = lax.broadcasted_iota(jnp.int32, (SAMPLE_ROWS, n_key), 0)
    key_col = lax.broadcasted_iota(jnp.int32, (1, n_key), 1)
    out = jnp.zeros((SAMPLE_ROWS, MOBA_HD), jnp.float32)
    out_row = lax.broadcasted_iota(jnp.int32, out.shape, 0)
    for t in range(t_real):
        keys = jnp.concatenate([ks_ref[slot, t].astype(bf16), kn], axis=0)
        vals = jnp.concatenate([vs_ref[slot, t].astype(bf16), vn], axis=0)
        bias = []
        for r in range(MOBA_TOPK):
            blk = idx_ref[(step * SAMPLE_ROWS + t) * MOBA_TOPK + r]
            bias.append(jnp.where(blk == last_blk, near[t:t + 1, :], far_bias))
        bias.append(own_bias[t:t + 1, :])
        s_all = lax.dot_general(q, keys, nt, preferred_element_type=jnp.float32)
        s_t = jnp.sum(jnp.where(row_sel == t, s_all, 0.0), axis=0, keepdims=True)
        own_col = key_col - MOBA_TOPK * MOBA_BLOCK
        visible = (own_col < 0) | ((own_col <= t) & (own_col < t_real))
        s_t = jnp.where(visible, s_t + jnp.concatenate(bias, axis=1), NEG_INF)
        m = jnp.max(s_t, axis=1, keepdims=True)
        p = jnp.exp2(s_t - m)
        den = jnp.sum(p, axis=1, keepdims=True)
        p8 = jnp.broadcast_to(p, (SAMPLE_ROWS, n_key)).astype(bf16)
        o_t = jnp.dot(p8, vals, preferred_element_type=jnp.float32) / den
        out = jnp.where(out_row == t, o_t, out)
    o_ref[...] = out.astype(o_ref.dtype)


def _sample_attn(page_table, sel_idx, rel_tbl, q_arr, q_blk0, k_all, v_all, cache_k, cache_v,
                 layer, t_real, out_dtype):
    ndb, n_pages = page_table.shape
    n = k_all.shape[1]
    past = n_pages * PAGE_SIZE
    last0 = past - MOBA_BLOCK
    t = np.arange(SAMPLE_ROWS)
    r = np.arange(MOBA_BLOCK)
    near_buckets = jnp.asarray(_rel_bucket_np(past + t[:, None] - (last0 + r[None, :])))
    grid_spec = pltpu.PrefetchScalarGridSpec(
        num_scalar_prefetch=3,
        grid=(ndb, MOBA_HEADS),
        in_specs=[
            pl.BlockSpec((SAMPLE_ROWS, MOBA_HD), lambda d, h, *_: (d, q_blk0 + h)),
            pl.BlockSpec((None, SAMPLE_ROWS, MOBA_HD), lambda d, h, *_: (layer, d, h)),
            pl.BlockSpec((None, SAMPLE_ROWS, MOBA_HD), lambda d, h, *_: (layer, d, h)),
            pl.BlockSpec((SAMPLE_ROWS, MOBA_BLOCK), lambda d, h, *_: (0, 0)),
            pl.BlockSpec(memory_space=pl.ANY),
            pl.BlockSpec(memory_space=pl.ANY),
        ],
        out_specs=pl.BlockSpec((SAMPLE_ROWS, MOBA_HD), lambda d, h, *_: (d, h)),
        scratch_shapes=[
            pltpu.VMEM((2, t_real, MOBA_TOPK * MOBA_BLOCK, MOBA_HD), jnp.float32),
            pltpu.VMEM((2, t_real, MOBA_TOPK * MOBA_BLOCK, MOBA_HD), jnp.float32),
            pltpu.SemaphoreType.DMA((2, 2)),
        ],
    )
    return pl.pallas_call(
        functools.partial(_sample_attn_kernel, n_pages, t_real, layer),
        out_shape=jax.ShapeDtypeStruct((n, D_MODEL), out_dtype),
        grid_spec=grid_spec,
        compiler_params=_cparams(("arbitrary", "arbitrary")),
        name="sample_attn",
    )(page_table.reshape(-1), sel_idx, rel_tbl, q_arr, k_all, v_all, near_buckets, cache_k,
      cache_v)


def _layer_norm(x, g, b):
    mu = jnp.mean(x, axis=-1, keepdims=True)
    xc = x - mu
    var = jnp.mean(xc * xc, axis=-1, keepdims=True)
    return xc * lax.rsqrt(var + LN_EPS) * g + b


def _mix_kernel(r_ref, m_ref, ga_ref, gb_ref, x_ref, wr_ref, wm_ref, wo_ref, g_ref, b_ref,
                wh_ref, wl_ref, x1_ref, ri_ref, rg_ref, cnt_ref, carry_ref):
    i = pl.program_id(0)

    @pl.when(i == 0)
    def _():
        carry_ref[...] = jnp.zeros_like(carry_ref)

    f32 = jnp.float32
    a = jnp.dot(r_ref[...].astype(jnp.bfloat16), wr_ref[...], preferred_element_type=f32)
    m = jnp.dot(m_ref[...].astype(jnp.bfloat16), wm_ref[...], preferred_element_type=f32)
    mix = _sigmoid(ga_ref[...].astype(f32)) * a + _sigmoid(gb_ref[...].astype(f32)) * m
    mixed = jnp.dot(mix.astype(jnp.bfloat16), wo_ref[...], preferred_element_type=f32)
    x1 = _layer_norm(DN_ALPHA * x_ref[...] + mixed, g_ref[...], b_ref[...])
    x1_ref[...] = x1

    xh = x1.astype(jnp.bfloat16)
    xl = (x1 - xh.astype(f32)).astype(jnp.bfloat16)
    logits = (jnp.dot(xh, wh_ref[...], preferred_element_type=f32)
              + jnp.dot(xl, wh_ref[...], preferred_element_type=f32)
              + jnp.dot(xh, wl_ref[...], preferred_element_type=f32))
    lane = lax.broadcasted_iota(jnp.int32, logits.shape, 1)
    is_g = lane < MOE_GROUPS
    lgm = jnp.where(is_g, logits, -jnp.inf)
    mg = jnp.max(lgm, axis=1, keepdims=True)
    grp = jnp.min(jnp.where(lgm == mg, lane, LANES), axis=1, keepdims=True)
    pg = 1.0 / jnp.sum(jnp.where(is_g, jnp.exp(logits - mg), 0.0), axis=1, keepdims=True)
    e_lane = lane - MOE_GROUPS
    in_grp = (e_lane >= 0) & (e_lane < MOE_EXPERTS) & ((e_lane >> 3) == grp)
    v1 = jnp.where(in_grp, logits, -jnp.inf)
    t1 = jnp.max(v1, axis=1, keepdims=True)
    i1 = jnp.min(jnp.where(v1 == t1, lane, LANES), axis=1, keepdims=True)
    v2 = jnp.where(lane == i1, -jnp.inf, v1)
    t2 = jnp.max(v2, axis=1, keepdims=True)
    i2 = jnp.min(jnp.where(v2 == t2, lane, LANES), axis=1, keepdims=True)
    z = jnp.exp(t2 - t1)
    g1 = pg / (1.0 + z)
    g2 = pg * z / (1.0 + z)
    oh = jnp.where((lane == i1) | (lane == i2), 1.0, 0.0)
    tm = oh.shape[0]
    tri = jnp.where(lax.broadcasted_iota(jnp.int32, (tm, tm), 0)
                    > lax.broadcasted_iota(jnp.int32, (tm, tm), 1), 1.0, 0.0)
    cum = jnp.dot(tri.astype(jnp.bfloat16), oh.astype(jnp.bfloat16),
                  preferred_element_type=f32) + carry_ref[...]
    r1 = jnp.sum(jnp.where(lane == i1, cum, 0.0), axis=1, keepdims=True)
    r2 = jnp.sum(jnp.where(lane == i2, cum, 0.0), axis=1, keepdims=True)
    carry_ref[...] = carry_ref[...] + jnp.sum(oh, axis=0, keepdims=True)
    ri = jnp.where(lane == 0, i1 - MOE_GROUPS,
                   jnp.where(lane == 1, i2 - MOE_GROUPS,
                             jnp.where(lane == 2, r1.astype(jnp.int32),
                                       jnp.where(lane == 3, r2.astype(jnp.int32), 0))))
    ri_ref[...] = ri
    rg_ref[...] = jnp.where(lane == 0, g1, jnp.where(lane == 1, g2, 0.0))
    cnt_ref[...] = carry_ref[...]


def _mix(ret_act, moba_o, gates, x, w_ret_o, w_moba_o, w_out, ln_g, ln_b, w_hi, w_lo, tm):
    n = x.shape[0]
    full = lambda shape: pl.BlockSpec(shape, lambda i: (0, 0))
    return pl.pallas_call(
        _mix_kernel,
        out_shape=(jax.ShapeDtypeStruct((n, D_MODEL), jnp.float32),
                   jax.ShapeDtypeStruct((n, LANES), jnp.int32),
                   jax.ShapeDtypeStruct((n, LANES), jnp.float32),
                   jax.ShapeDtypeStruct((1, LANES), jnp.float32)),
        grid=(n // tm,),
        in_specs=[
            pl.BlockSpec((tm, RET_W), lambda i: (i, 0)),
            pl.BlockSpec((tm, D_MODEL), lambda i: (i, 0)),
            pl.BlockSpec((tm, D_MODEL), lambda i: (i, 0)),
            pl.BlockSpec((tm, D_MODEL), lambda i: (i, 1)),
            pl.BlockSpec((tm, D_MODEL), lambda i: (i, 0)),
            full((RET_W, D_MODEL)), full((D_MODEL, D_MODEL)), full((D_MODEL, D_MODEL)),
            full((1, D_MODEL)), full((1, D_MODEL)),
            full((D_MODEL, LANES)), full((D_MODEL, LANES)),
        ],
        out_specs=(pl.BlockSpec((tm, D_MODEL), lambda i: (i, 0)),
                   pl.BlockSpec((tm, LANES), lambda i: (i, 0)),
                   pl.BlockSpec((tm, LANES), lambda i: (i, 0)),
                   pl.BlockSpec((1, LANES), lambda i: (0, 0))),
        scratch_shapes=[pltpu.VMEM((1, LANES), jnp.float32)],
        compiler_params=_cparams(("arbitrary",)),
        name="mix_ln_router",
    )(ret_act, moba_o, gates, gates, x, w_ret_o, w_moba_o, w_out, ln_g, ln_b, w_hi, w_lo)


def _load_dest(dest_hbm, dest_ref, sem):
    cp = pltpu.make_async_copy(dest_hbm.at[pl.program_id(0)], dest_ref, sem)
    cp.start()
    cp.wait()


def _dest_at(dest_ref, t, k):
    e = 2 * t + k
    return dest_ref[lax.shift_right_logical(e, 7), lax.bitwise_and(e, LANES - 1)]


def _dispatch_kernel(tm, dest_hbm, x_ref, z_hbm, o_hbm, dest_ref, sem, isem):
    del z_hbm
    _load_dest(dest_hbm, dest_ref, isem)

    def row_copy(t, k):
        return pltpu.make_async_copy(x_ref.at[pl.ds(t, 1)],
                                     o_hbm.at[pl.ds(_dest_at(dest_ref, t, k), 1)], sem)

    def start(t, carry):
        row_copy(t, 0).start()
        row_copy(t, 1).start()
        return carry

    def wait(t, carry):
        row_copy(t, 0).wait()
        row_copy(t, 1).wait()
        return carry

    lax.fori_loop(0, tm, start, 0, unroll=ROW_DMA_UNROLL)
    lax.fori_loop(0, tm, wait, 0, unroll=ROW_DMA_UNROLL)


def _dispatch(x1, dest, n_rows, tm):
    n = x1.shape[0]
    return pl.pallas_call(
        functools.partial(_dispatch_kernel, tm),
        out_shape=jax.ShapeDtypeStruct((n_rows, D_MODEL), jnp.float32),
        grid=(n // tm,),
        in_specs=[pl.BlockSpec(memory_space=pl.ANY),
                  pl.BlockSpec((tm, D_MODEL), lambda i: (i, 0)),
                  pl.BlockSpec(memory_space=pl.ANY)],
        out_specs=pl.BlockSpec(memory_space=pl.ANY),
        scratch_shapes=[pltpu.SMEM((2 * tm // LANES, LANES), jnp.int32),
                        pltpu.SemaphoreType.DMA, pltpu.SemaphoreType.DMA],
        input_output_aliases={2: 0},
        compiler_params=_cparams(("arbitrary",)),
        name="moe_dispatch",
    )(dest, x1, jnp.zeros((n_rows, D_MODEL), jnp.float32))


def _experts_kernel(be_ref, nu_ref, x_ref, wg_ref, wu_ref, wd_ref, o_ref, wgb, wub, wdb):
    i = pl.program_id(0)
    used = i < nu_ref[0]
    first = (i == 0) | (be_ref[i] != be_ref[jnp.maximum(i - 1, 0)])

    @pl.when(used & first)
    def _():
        wgb[...] = wg_ref[...].astype(jnp.bfloat16)
        wub[...] = wu_ref[...].astype(jnp.bfloat16)
        wdb[...] = wd_ref[...].astype(jnp.bfloat16)

    @pl.when(used)
    def _():
        xb = x_ref[...].astype(jnp.bfloat16)
        g = jnp.dot(xb, wgb[...], preferred_element_type=jnp.float32)
        u = jnp.dot(xb, wub[...], preferred_element_type=jnp.float32)
        hid = (g * _sigmoid(g) * u).astype(jnp.bfloat16)
        o_ref[...] = jnp.dot(hid, wdb[...], preferred_element_type=jnp.float32)

    @pl.when(jnp.logical_not(used))
    def _():
        o_ref[...] = jnp.zeros_like(o_ref)


def _experts(xs, blk_exp, n_used, w_gate, w_up, w_down, layer):
    n_rows = xs.shape[0]
    r = MOE_ROWS
    grid_spec = pltpu.PrefetchScalarGridSpec(
        num_scalar_prefetch=2,
        grid=(n_rows // r,),
        in_specs=[
            pl.BlockSpec((r, D_MODEL), lambda i, be, nu: (i, 0)),
            pl.BlockSpec((None, None, D_MODEL, MOE_FF), lambda i, be, nu: (layer, be[i], 0, 0)),
            pl.BlockSpec((None, None, D_MODEL, MOE_FF), lambda i, be, nu: (layer, be[i], 0, 0)),
            pl.BlockSpec((None, None, MOE_FF, D_MODEL), lambda i, be, nu: (layer, be[i], 0, 0)),
        ],
        out_specs=pl.BlockSpec((r, D_MODEL), lambda i, be, nu: (i, 0)),
        scratch_shapes=[pltpu.VMEM((D_MODEL, MOE_FF), jnp.bfloat16),
                        pltpu.VMEM((D_MODEL, MOE_FF), jnp.bfloat16),
                        pltpu.VMEM((MOE_FF, D_MODEL), jnp.bfloat16)],
    )
    return pl.pallas_call(
        _experts_kernel,
        out_shape=jax.ShapeDtypeStruct((n_rows, D_MODEL), jnp.float32),
        grid_spec=grid_spec,
        compiler_params=_cparams(("arbitrary",)),
        name="moe_experts",
    )(blk_exp, n_used, xs, w_gate, w_up, w_down)


def _combine_kernel(tm, dest_hbm, x_ref, gt_ref, g_ref, b_ref, y_hbm, o_ref, ob_ref,
                    ya_ref, yb_ref, dest_ref, sem, isem):
    _load_dest(dest_hbm, dest_ref, isem)

    def row_copy(t, k):
        buf = ya_ref if k == 0 else yb_ref
        return pltpu.make_async_copy(y_hbm.at[pl.ds(_dest_at(dest_ref, t, k), 1)],
                                     buf.at[pl.ds(t, 1)], sem.at[k])

    def start(t, carry):
        row_copy(t, 0).start()
        row_copy(t, 1).start()
        return carry

    def wait(t, carry):
        row_copy(t, 0).wait()
        row_copy(t, 1).wait()
        return carry

    lax.fori_loop(0, tm, start, 0, unroll=ROW_DMA_UNROLL)
    lax.fori_loop(0, tm, wait, 0, unroll=ROW_DMA_UNROLL)
    gt = gt_ref[...]
    y = gt[:, 0:1] * ya_ref[...] + gt[:, 1:2] * yb_ref[...]
    x2 = _layer_norm(DN_ALPHA * x_ref[...] + y, g_ref[...], b_ref[...])
    o_ref[...] = x2
    ob_ref[...] = x2.astype(jnp.bfloat16)


def _combine(x1, gates, dest, ys, ln_g, ln_b, tm):
    n = x1.shape[0]
    return pl.pallas_call(
        functools.partial(_combine_kernel, tm),
        out_shape=(jax.ShapeDtypeStruct((n, D_MODEL), jnp.float32),
                   jax.ShapeDtypeStruct((n, D_MODEL), jnp.bfloat16)),
        grid=(n // tm,),
        in_specs=[pl.BlockSpec(memory_space=pl.ANY),
                  pl.BlockSpec((tm, D_MODEL), lambda i: (i, 0)),
                  pl.BlockSpec((tm, LANES), lambda i: (i, 0)),
                  pl.BlockSpec((1, D_MODEL), lambda i: (0, 0)),
                  pl.BlockSpec((1, D_MODEL), lambda i: (0, 0)),
                  pl.BlockSpec(memory_space=pl.ANY)],
        out_specs=(pl.BlockSpec((tm, D_MODEL), lambda i: (i, 0)),
                   pl.BlockSpec((tm, D_MODEL), lambda i: (i, 0))),
        scratch_shapes=[pltpu.VMEM((tm, D_MODEL), jnp.float32),
                        pltpu.VMEM((tm, D_MODEL), jnp.float32),
                        pltpu.SMEM((2 * tm // LANES, LANES), jnp.int32),
                        pltpu.SemaphoreType.DMA((2,)), pltpu.SemaphoreType.DMA],
        compiler_params=_cparams(("arbitrary",)),
        name="moe_combine_ln",
    )(dest, x1, gates, ln_g, ln_b, ys)


def _moe(x1, route_i, route_g, counts, w_gate, w_up, w_down, layer, ln_g, ln_b, tm):
    n = x1.shape[0]
    r = MOE_ROWS
    n_blk = (2 * n) // r + MOE_EXPERTS
    cnt = counts[0, MOE_GROUPS:MOE_GROUPS + MOE_EXPERTS].astype(jnp.int32)
    padded = (cnt + r - 1) // r * r
    pad_end = jnp.cumsum(padded)
    pad_start = pad_end - padded
    dest = (pad_start[route_i[:, 0:2]] + route_i[:, 2:4]).astype(jnp.int32)
    dest = dest.reshape(n // tm, 2 * tm // LANES, LANES)
    blk_start = jnp.arange(n_blk, dtype=jnp.int32) * r
    blk_exp = jnp.minimum(jnp.sum(pad_end[None, :] <= blk_start[:, None], axis=1),
                          MOE_EXPERTS - 1).astype(jnp.int32)
    n_used = (pad_end[-1:] // r).astype(jnp.int32)
    xs = _dispatch(x1, dest, n_blk * r, tm)
    ys = _experts(xs, blk_exp, n_used, w_gate, w_up, w_down, layer)
    return _combine(x1, route_g, dest, ys, ln_g, ln_b, tm)


def _split_hi_lo(w):
    hi = w.astype(jnp.bfloat16)
    return hi, (w - hi.astype(jnp.float32)).astype(jnp.bfloat16)


def _layer_common(x, xb, k_prev, v_prev, attn_fn, lw, layer, tm_proj, tm, tm_moe, act_dtype):
    (w_in, col_scale, w_ret_o, w_moba_o, w_out, ln1_g, ln1_b, w_hi, w_lo,
     w_gate, w_up, w_down, ln2_g, ln2_b) = lw
    proj = functools.partial(_proj, xb, w_in, layer, col_scale)
    qk = proj(COL_RQ, 2, jnp.float32, tm_proj)
    vgq = proj(COL_RV, 5, act_dtype, tm_proj)
    k_all = _proj_into(xb, w_in, layer, col_scale, COL_MK, k_prev, tm_proj)
    v_all = _proj_into(xb, w_in, layer, col_scale, COL_MK + 1, v_prev, tm_proj)
    gates = proj(COL_GA, 2, act_dtype, tm_proj)
    ret_act, ret_state, moba_o = attn_fn(qk, vgq, k_all, v_all)
    x1, route_i, route_g, counts = _mix(ret_act, moba_o, gates, x, w_ret_o, w_moba_o, w_out,
                                        ln1_g, ln1_b, w_hi, w_lo, tm)
    x2, x2b = _moe(x1, route_i, route_g, counts, w_gate, w_up, w_down, layer, ln2_g, ln2_b,
                   tm_moe)
    return x2, x2b, k_all, v_all, ret_state


def kernel(x_prompt, x_sample, cache_k, cache_v, state_ret, page_table, rel_bias, w_in, ret_gn_w,
           w_ret_o, w_moba_o, w_out, ln1_g, ln1_b, w_group, w_router, w_exp_gate, w_exp_up,
           w_exp_down, ln2_g, ln2_b):
    b, s, _ = x_prompt.shape
    db, t_real, _ = x_sample.shape
    n_pages = page_table.shape[1]
    past = n_pages * PAGE_SIZE
    bf16 = jnp.bfloat16

    col_scale = np.ones((1, PROJ_TILES * D_MODEL), np.float32)
    col_scale[0, D_MODEL:2 * D_MODEL] = RET_DK ** -0.5
    col_scale[0, COL_MQ * D_MODEL:(COL_MQ + 1) * D_MODEL] = MOBA_HD ** -0.5 * LOG2E
    col_scale = jnp.asarray(col_scale)
    rel_tbl = (rel_bias.T * LOG2E).reshape(-1)

    cos_p, sin_p = _xpos_tables(jnp.arange(s))
    pos_s = jnp.minimum(past + jnp.arange(SAMPLE_ROWS), past + t_real - 1)
    cos_s, sin_s = _xpos_tables(pos_s)

    xp = x_prompt.reshape(b * s, D_MODEL)
    xs = jnp.pad(x_sample, ((0, 0), (0, SAMPLE_ROWS - t_real), (0, 0))).reshape(
        db * SAMPLE_ROWS, D_MODEL)
    xpb, xsb = xp.astype(bf16), xs.astype(bf16)
    psums = _page_sums(cache_k)
    zero_state = jnp.zeros((b, RET_HEADS, RET_DK, RET_DV), jnp.float32)
    state_all = state_ret.reshape(DEPTH * db, RET_HEADS, RET_DK, RET_DV)
    mq_blk = (COL_MQ - COL_RV) * D_MODEL // MOBA_HD

    rp = []
    kp, vp = (jnp.zeros((DEPTH, b * s, D_MODEL), jnp.float32) for _ in range(2))
    ks, vs = (jnp.zeros((DEPTH, db * SAMPLE_ROWS, D_MODEL), jnp.float32) for _ in range(2))
    st_s = jnp.zeros((DEPTH * db, RET_HEADS, RET_DK, RET_DV), jnp.float32)
    for l in range(DEPTH):
        w_gr = jnp.zeros((D_MODEL, LANES), jnp.float32)
        w_gr = w_gr.at[:, :MOE_GROUPS].set(w_group[l])
        w_gr = w_gr.at[:, MOE_GROUPS:MOE_GROUPS + MOE_EXPERTS].set(w_router[l])
        w_hi, w_lo = _split_hi_lo(w_gr)
        lw = (w_in, col_scale, w_ret_o[l].astype(bf16), w_moba_o[l].astype(bf16),
              w_out[l].astype(bf16), ln1_g[l][None], ln1_b[l][None], w_hi, w_lo,
              w_exp_gate, w_exp_up, w_exp_down, ln2_g[l][None], ln2_b[l][None])
        gn = ret_gn_w[l][None]

        def prompt_attn(qk, vgq, k_all, v_all, l=l):
            ret_act, st = _retention(qk, vgq, 0, 1, cos_p, sin_p, gn, zero_state, 0,
                                     b, s // RET_CHUNK, RET_CHUNK, RET_CHUNK, bf16)
            moba_o = _moba_prompt(vgq, mq_blk, k_all, v_all, l, rel_tbl, b, s, bf16)
            return ret_act, st, moba_o

        def sample_attn(qk, vgq, k_all, v_all, l=l, st_prev=st_s):
            ret_act, st = _retention(qk, vgq, 0, 1, cos_s, sin_s, gn, state_all, l * db,
                                     db, 1, SAMPLE_ROWS, t_real, jnp.float32,
                                     sf_rows=DEPTH * db, sf_off=l * db, sf_prev=st_prev)
            sel = _sample_select(page_table, vgq, COL_MQ - COL_RV, psums, l)
            sel_idx = sel[:, :, :, :MOBA_TOPK].reshape(-1)
            moba_o = _sample_attn(page_table, sel_idx, rel_tbl, vgq, mq_blk, k_all, v_all,
                                  cache_k, cache_v, l, t_real, jnp.float32)
            return ret_act, st, moba_o

        n_s = db * SAMPLE_ROWS
        xp, xpb, kp, vp, st_p = _layer_common(xp, xpb, kp, vp, prompt_attn, lw, l, 512, 256, 512,
                                              bf16)
        tm_s = min(256, n_s)
        xs, xsb, ks, vs, st_s = _layer_common(xs, xsb, ks, vs, sample_attn, lw, l, tm_s, tm_s,
                                              tm_s, jnp.float32)
        rp.append(st_p)

    page_shape = (DEPTH, b, s // PAGE_SIZE, PAGE_SIZE, MOBA_HEADS, MOBA_HD)
    new_shape = (DEPTH, db, SAMPLE_ROWS, MOBA_HEADS, MOBA_HD)
    yp = xp.reshape(b, s, D_MODEL)
    ys = xs.reshape(db, SAMPLE_ROWS, D_MODEL)[:, :t_real]
    return (yp, ys, kp.reshape(page_shape), vp.reshape(page_shape), jnp.stack(rp),
            ks.reshape(new_shape)[:, :, :t_real], vs.reshape(new_shape)[:, :, :t_real],
            st_s.reshape(DEPTH, db, RET_HEADS, RET_DK, RET_DV))
```

```python
import functools
import math

import numpy as np
import jax
import jax.numpy as jnp
from jax import lax
from jax.experimental import pallas as pl
from jax.experimental.pallas import tpu as pltpu

D_MODEL = 1024
DEPTH = 2
PAGE_SIZE = 128
RET_HEADS = 4
RET_DK = D_MODEL // RET_HEADS
RET_DV = 2 * D_MODEL // RET_HEADS
RET_W = RET_HEADS * RET_DV
RET_CHUNK = 128
XPOS_BASE = 10000.0
GN_EPS = 1e-6
MOBA_HEADS = 8
MOBA_HD = D_MODEL // MOBA_HEADS
MOBA_BLOCK = 256
MOBA_TOPK = 3
REL_BUCKETS = 32
REL_MAX_DIST = 128
MOE_GROUPS = 4
MOE_EPG = 8
MOE_EXPERTS = MOE_GROUPS * MOE_EPG
MOE_FF = D_MODEL // 2
DN_ALPHA = (2.0 * DEPTH) ** 0.25
LN_EPS = 1e-5
NEG_INF = -1e30
LOG2E = math.log2(math.e)

COL_RQ, COL_RV, COL_RG, COL_MQ, COL_MK, COL_GA = 0, 2, 4, 6, 7, 9
PROJ_TILES = 11

LANES = 128
SUBLANES = 8
VMEM_LIMIT = 56 * 1024 * 1024
SAMPLE_ROWS = 8
MOE_ROWS = 256
ROW_DMA_UNROLL = 8
MOBA_ONES_ROWS = 16
MOBA_GROUP = 4


def _cparams(sem):
    return pltpu.CompilerParams(dimension_semantics=sem, vmem_limit_bytes=VMEM_LIMIT)


def _sigmoid(x):
    return 1.0 / (1.0 + jnp.exp(-x))


def _proj_kernel(x_ref, w_ref, s_ref, o_ref, wb_ref):
    @pl.when(pl.program_id(1) == 0)
    def _():
        wb_ref[...] = w_ref[...].astype(jnp.bfloat16)

    acc = jnp.dot(x_ref[...], wb_ref[...], preferred_element_type=jnp.float32)
    o_ref[...] = (acc * s_ref[...]).astype(o_ref.dtype)


def _proj(xb, w_in, layer, col_scale, col0, ncols, out_dtype, tm):
    n = xb.shape[0]
    tn = D_MODEL
    return pl.pallas_call(
        _proj_kernel,
        out_shape=jax.ShapeDtypeStruct((n, ncols * tn), out_dtype),
        grid=(ncols, n // tm),
        in_specs=[
            pl.BlockSpec((tm, D_MODEL), lambda j, i: (i, 0)),
            pl.BlockSpec((None, D_MODEL, tn), lambda j, i: (layer, 0, j + col0)),
            pl.BlockSpec((1, tn), lambda j, i: (0, j + col0)),
        ],
        out_specs=pl.BlockSpec((tm, tn), lambda j, i: (i, j)),
        scratch_shapes=[pltpu.VMEM((D_MODEL, tn), jnp.bfloat16)],
        compiler_params=_cparams(("arbitrary", "arbitrary")),
        name="in_proj",
    )(xb, w_in, col_scale)


def _proj_layer_kernel(x_ref, w_ref, s_ref, prev_ref, o_ref, wb_ref):
    del prev_ref
    _proj_kernel(x_ref, w_ref, s_ref, o_ref, wb_ref)


def _proj_into(xb, w_in, layer, col_scale, col, prev, tm):
    n = xb.shape[0]
    tn = D_MODEL
    return pl.pallas_call(
        _proj_layer_kernel,
        out_shape=jax.ShapeDtypeStruct((DEPTH, n, tn), jnp.float32),
        grid=(1, n // tm),
        in_specs=[
            pl.BlockSpec((tm, D_MODEL), lambda j, i: (i, 0)),
            pl.BlockSpec((None, D_MODEL, tn), lambda j, i: (layer, 0, col)),
            pl.BlockSpec((1, tn), lambda j, i: (0, col)),
            pl.BlockSpec(memory_space=pl.ANY),
        ],
        out_specs=pl.BlockSpec((None, tm, tn), lambda j, i: (layer, i, 0)),
        scratch_shapes=[pltpu.VMEM((D_MODEL, tn), jnp.bfloat16)],
        input_output_aliases={3: 0},
        compiler_params=_cparams(("arbitrary", "arbitrary")),
        name="in_proj_kv",
    )(xb, w_in, col_scale, prev)


def _ret_consts(c, c_true):
    lg = np.log1p(-np.exp2(-5.0 - np.arange(RET_HEADS, dtype=np.float64)))
    j = np.arange(c, dtype=np.float64)
    diff = j[:, None] - j[None, :]
    dmask = np.where(diff >= 0, np.exp(lg[:, None, None] * np.maximum(diff, 0.0)), 0.0)
    qdec = np.exp(lg[:, None] * (j + 1.0))[..., None]
    kdec = np.where(j < c_true, np.exp(lg[:, None] * (c_true - 1.0 - j)), 0.0)[..., None]
    sdec = [float(v) for v in np.exp(lg * c_true)]
    return (jnp.asarray(dmask, jnp.float32), jnp.asarray(qdec, jnp.float32),
            jnp.asarray(kdec, jnp.float32), sdec)


def _ret_kernel(sdec, q_ref, k_ref, v_ref, g_ref, cos_ref, sin_ref, dm_ref, qd_ref, kd_ref,
                gn_ref, s0_ref, o_ref, sf_ref, st_ref):
    c = pl.program_id(1)

    @pl.when(c == 0)
    def _():
        st_ref[...] = s0_ref[0]

    cos = cos_ref[...]
    sin = sin_ref[...]
    half = RET_DK // 2

    def rot(x):
        x1, x2 = x[:, :half], x[:, half:]
        return jnp.concatenate([x1 * cos - x2 * sin, x1 * sin + x2 * cos], axis=-1)

    for h in range(RET_HEADS):
        qr = rot(q_ref[:, h * RET_DK:(h + 1) * RET_DK])
        kr = rot(k_ref[:, h * RET_DK:(h + 1) * RET_DK])
        qb = qr.astype(jnp.bfloat16)
        kb = kr.astype(jnp.bfloat16)
        v = v_ref[:, h * RET_DV:(h + 1) * RET_DV].astype(jnp.bfloat16)
        st = st_ref[h]
        s = lax.dot_general(qb, kb, (((1,), (1,)), ((), ())),
                            preferred_element_type=jnp.float32) * dm_ref[h]
        inner = jnp.dot(s.astype(jnp.bfloat16), v, preferred_element_type=jnp.float32)
        cross = jnp.dot(qb, st.astype(jnp.bfloat16),
                        preferred_element_type=jnp.float32) * qd_ref[h]
        kdb = (kr * kd_ref[h]).astype(jnp.bfloat16)
        st_ref[h] = sdec[h] * st + lax.dot_general(
            kdb, v, (((0,), (0,)), ((), ())), preferred_element_type=jnp.float32)
        o = inner + cross
        mu = jnp.mean(o, axis=-1, keepdims=True)
        oc = o - mu
        var = jnp.mean(oc * oc, axis=-1, keepdims=True)
        on = oc * lax.rsqrt(var + GN_EPS) * gn_ref[:, h * RET_DV:(h + 1) * RET_DV]
        g = g_ref[:, h * RET_DV:(h + 1) * RET_DV].astype(jnp.float32)
        o_ref[:, h * RET_DV:(h + 1) * RET_DV] = (on * (g * _sigmoid(g))).astype(o_ref.dtype)

    @pl.when(c == pl.num_programs(1) - 1)
    def _():
        sf_ref[0] = st_ref[...]


def _ret_layer_kernel(sdec, *refs):
    _ret_kernel(sdec, *refs[:11], *refs[12:])


def _retention(qk, vg, v_blk, g_blk, cos, sin, gn_w, state0, s0_off, nb, nc, c, c_true,
               out_dtype, sf_rows=None, sf_off=0, sf_prev=None):
    n = qk.shape[0]
    sf_rows = nb if sf_rows is None else sf_rows
    dmask, qdec, kdec, sdec = _ret_consts(c, c_true)
    row = lambda b, i: b * nc + i
    extra_specs = [] if sf_prev is None else [pl.BlockSpec(memory_space=pl.ANY)]
    extra_args = [] if sf_prev is None else [sf_prev]
    return pl.pallas_call(
        functools.partial(_ret_kernel if sf_prev is None else _ret_layer_kernel, sdec),
        out_shape=(jax.ShapeDtypeStruct((n, RET_W), out_dtype),
                   jax.ShapeDtypeStruct((sf_rows, RET_HEADS, RET_DK, RET_DV), jnp.float32)),
        grid=(nb, nc),
        input_output_aliases={} if sf_prev is None else {11: 1},
        in_specs=[
            pl.BlockSpec((c, D_MODEL), lambda b, i: (row(b, i), 0)),
            pl.BlockSpec((c, D_MODEL), lambda b, i: (row(b, i), 1)),
            pl.BlockSpec((c, RET_W), lambda b, i: (row(b, i), v_blk)),
            pl.BlockSpec((c, RET_W), lambda b, i: (row(b, i), g_blk)),
            pl.BlockSpec((c, RET_DK // 2), lambda b, i: (i, 0)),
            pl.BlockSpec((c, RET_DK // 2), lambda b, i: (i, 0)),
            pl.BlockSpec((RET_HEADS, c, c), lambda b, i: (0, 0, 0)),
            pl.BlockSpec((RET_HEADS, c, 1), lambda b, i: (0, 0, 0)),
            pl.BlockSpec((RET_HEADS, c, 1), lambda b, i: (0, 0, 0)),
            pl.BlockSpec((1, RET_W), lambda b, i: (0, 0)),
            pl.BlockSpec((1, RET_HEADS, RET_DK, RET_DV), lambda b, i: (s0_off + b, 0, 0, 0)),
        ] + extra_specs,
        out_specs=(pl.BlockSpec((c, RET_W), lambda b, i: (row(b, i), 0)),
                   pl.BlockSpec((1, RET_HEADS, RET_DK, RET_DV),
                                lambda b, i: (sf_off + b, 0, 0, 0))),
        scratch_shapes=[pltpu.VMEM((RET_HEADS, RET_DK, RET_DV), jnp.float32)],
        compiler_params=_cparams(("arbitrary", "arbitrary")),
        name="retention",
    )(qk, qk, vg, vg, cos, sin, dmask, qdec, kdec, gn_w, state0, *extra_args)


def _xpos_tables(pos):
    half = RET_DK // 2
    inv = 1.0 / (XPOS_BASE ** jnp.linspace(0.0, 1.0, half, dtype=jnp.float32))
    ang = pos.astype(jnp.float32)[:, None] * inv[None, :]
    return jnp.cos(ang), jnp.sin(ang)


def _rel_bucket_np(rel):
    n = np.maximum(rel, 0)
    exact = REL_BUCKETS // 2
    nf = np.maximum(n, 1).astype(np.float32)
    large = exact + (np.log(nf / np.float32(exact)) / np.float32(math.log(REL_MAX_DIST / exact))
                     * np.float32(REL_BUCKETS - exact)).astype(np.int32)
    return np.where(n < exact, n, np.minimum(large, REL_BUCKETS - 1)).astype(np.int32)


def _bias_from_buckets(bucket, table_ref, h):
    out = jnp.zeros(bucket.shape, jnp.float32)
    for b in range(REL_BUCKETS):
        out = jnp.where(bucket == b, table_ref[h * REL_BUCKETS + b], out)
    return out


def _top3_mask(scores, valid, row, nrow):
    sc = jnp.where(valid, scores, NEG_INF)
    sel = jnp.zeros(scores.shape, jnp.float32)
    for _ in range(MOBA_TOPK):
        m = jnp.max(sc, axis=0, keepdims=True)
        idx = jnp.min(jnp.where(sc == m, row, nrow), axis=0, keepdims=True)
        pick = row == idx
        sel = jnp.where(pick, 1.0, sel)
        sc = jnp.where(pick, -jnp.inf, sc)
    return jnp.where(valid, sel, 0.0)


def _moba_prompt_kernel(nblk, grp, tbl_ref, q_ref, k_ref, v_ref, bk_ref, c_ref, o_ref, ps_ref,
                        kb_ref, vt_ref, mh_ref, ml_ref, bown_ref, bprev_ref, sel_ref, acc_ref,
                        sa_ref, sb_ref):
    h = pl.program_id(1)
    qi = pl.program_id(2)
    blk = MOBA_BLOCK
    bf16 = jnp.bfloat16
    f32 = jnp.float32
    nt = (((1,), (1,)), ((), ()))

    ps_ref[...] = jnp.sum(c_ref[...], axis=1)

    @pl.when(qi == 0)
    def _():
        kf = k_ref[...]
        kb_ref[...] = kf.astype(bf16)
        means = jnp.mean(kf.reshape(nblk, blk, MOBA_HD), axis=1)
        mh = means.astype(bf16)
        mh_ref[...] = mh
        ml_ref[...] = (means - mh.astype(f32)).astype(bf16)

        def transpose_block(j, carry):
            r0 = pl.multiple_of(j * blk, blk)
            vt_ref[j, :MOBA_HD, :] = v_ref[pl.ds(r0, blk), :].T.astype(bf16)
            vt_ref[j, MOBA_HD:, :] = jnp.ones((MOBA_ONES_ROWS, blk), bf16)
            return carry

        lax.fori_loop(0, nblk, transpose_block, 0)
        far = tbl_ref[h * REL_BUCKETS + REL_BUCKETS - 1]
        bown_ref[...] = _bias_from_buckets(bk_ref[0], tbl_ref, h) - far
        bprev_ref[...] = _bias_from_buckets(bk_ref[1], tbl_ref, h) - far

    q = q_ref[...]
    tq = q.shape[0]
    scores = (lax.dot_general(mh_ref[...], q, nt, preferred_element_type=f32)
              + lax.dot_general(ml_ref[...], q, nt, preferred_element_type=f32))
    row = lax.broadcasted_iota(jnp.int32, scores.shape, 0)
    sel_ref[...] = _top3_mask(scores, row < qi, row, nblk)

    def softmax_step(tiles, m_old):
        m_new = m_old
        for s, keep, _ in tiles:
            m_new = jnp.maximum(m_new, jnp.where(keep, jnp.max(s, axis=0, keepdims=True), NEG_INF))
        upd = None
        for s, keep, vt in tiles:
            p = jnp.exp2(s - jnp.where(keep, m_new, -NEG_INF))
            d = jnp.dot(vt, p.astype(bf16), preferred_element_type=f32)
            upd = d if upd is None else upd + d
        acc_ref[...] = jnp.exp2(m_old - m_new) * acc_ref[...] + upd
        return m_new

    def picked(n, ok):
        return jnp.where(ok, sel_ref[pl.ds(n, 1), :], 0.0) > 0.0

    def group_base(gi):
        return jnp.minimum(gi * grp, nblk - grp)

    def group_logits(gi):
        r0 = pl.multiple_of(group_base(gi) * blk, blk)
        return lax.dot_general(kb_ref[pl.ds(r0, grp * blk), :], q, nt, preferred_element_type=f32)

    sa_ref[...] = group_logits(0)

    acc_ref[...] = jnp.zeros_like(acc_ref)
    prev = jnp.maximum(qi - 1, 0)
    s_prev = lax.dot_general(kb_ref[pl.ds(pl.multiple_of(prev * blk, blk), blk), :], q, nt,
                             preferred_element_type=f32) + bprev_ref[...]
    s_own = lax.dot_general(kb_ref[pl.ds(pl.multiple_of(qi * blk, blk), blk), :], q, nt,
                            preferred_element_type=f32) + bown_ref[...]
    k_i = lax.broadcasted_iota(jnp.int32, s_own.shape, 0)
    q_i = lax.broadcasted_iota(jnp.int32, s_own.shape, 1)
    s_own = jnp.where(q_i >= k_i, s_own, NEG_INF)
    everyone = jnp.full((1, tq), True)
    m = softmax_step([(s_prev, picked(prev, qi >= 1), vt_ref[prev]),
                      (s_own, everyone, vt_ref[qi])], jnp.full((1, tq), NEG_INF, f32))

    n_far = jnp.maximum(qi - 1, 0)

    def far_group(gi, s_ref, m_old):
        first = gi * grp
        base = group_base(gi)
        tiles = []
        for i in range(grp):
            n = base + i
            tiles.append((s_ref[i * blk:(i + 1) * blk, :],
                          picked(n, (n >= first) & (n < n_far)), vt_ref[n]))
        return softmax_step(tiles, m_old)

    def far_pair(j, m_old):
        sb_ref[...] = group_logits(2 * j + 1)
        m_mid = far_group(2 * j, sa_ref, m_old)
        sa_ref[...] = group_logits(2 * j + 2)
        return far_group(2 * j + 1, sb_ref, m_mid)

    n_grp = (n_far + grp - 1) // grp
    lax.fori_loop(0, (n_grp + 1) // 2, far_pair, m)
    acc = acc_ref[...]
    o_ref[...] = (acc[:MOBA_HD] / acc[MOBA_HD:MOBA_HD + 1]).T.astype(o_ref.dtype)


def _moba_prompt(q_arr, q_blk0, k_all, v_all, layer, rel_tbl, cache_k, nb, s, out_dtype):
    n = k_all.shape[1]
    blk = MOBA_BLOCK
    nblk = s // blk
    grp = min(MOBA_GROUP, nblk)
    n_phys = cache_k.shape[1]
    ppstep = n_phys // (nb * MOBA_HEADS * nblk)
    assert ppstep * nb * MOBA_HEADS * nblk == n_phys, "cache pages must split evenly over steps"
    step = lambda b, h, qi: (b * MOBA_HEADS + h) * nblk + qi
    i = np.arange(blk)
    buckets = jnp.asarray(np.stack([_rel_bucket_np(i[None, :] - i[:, None]),
                                    _rel_bucket_np(blk + i[None, :] - i[:, None])]))
    grid_spec = pltpu.PrefetchScalarGridSpec(
        num_scalar_prefetch=1,
        grid=(nb, MOBA_HEADS, nblk),
        in_specs=[
            pl.BlockSpec((blk, MOBA_HD), lambda b, h, qi, t: (b * nblk + qi, q_blk0 + h)),
            pl.BlockSpec((None, s, MOBA_HD), lambda b, h, qi, t: (layer, b, h)),
            pl.BlockSpec((None, s, MOBA_HD), lambda b, h, qi, t: (layer, b, h)),
            pl.BlockSpec((2, blk, blk), lambda b, h, qi, t: (0, 0, 0)),
            pl.BlockSpec((None, ppstep, PAGE_SIZE, MOBA_HEADS, MOBA_HD),
                         lambda b, h, qi, t: (layer, step(b, h, qi), 0, 0, 0)),
        ],
        out_specs=(pl.BlockSpec((blk, MOBA_HD), lambda b, h, qi, t: (b * nblk + qi, h)),
                   pl.BlockSpec((ppstep, MOBA_HEADS, MOBA_HD),
                                lambda b, h, qi, t: (step(b, h, qi), 0, 0))),
        scratch_shapes=[
            pltpu.VMEM((s, MOBA_HD), jnp.bfloat16),
            pltpu.VMEM((nblk, MOBA_HD + MOBA_ONES_ROWS, blk), jnp.bfloat16),
            pltpu.VMEM((nblk, MOBA_HD), jnp.bfloat16),
            pltpu.VMEM((nblk, MOBA_HD), jnp.bfloat16),
            pltpu.VMEM((blk, blk), jnp.float32),
            pltpu.VMEM((blk, blk), jnp.float32),
            pltpu.VMEM((nblk, blk), jnp.float32),
            pltpu.VMEM((MOBA_HD + MOBA_ONES_ROWS, blk), jnp.float32),
            pltpu.VMEM((grp * blk, blk), jnp.float32),
            pltpu.VMEM((grp * blk, blk), jnp.float32),
        ],
    )
    return pl.pallas_call(
        functools.partial(_moba_prompt_kernel, nblk, grp),
        out_shape=(jax.ShapeDtypeStruct((n, D_MODEL), out_dtype),
                   jax.ShapeDtypeStruct((n_phys, MOBA_HEADS, MOBA_HD), jnp.float32)),
        grid_spec=grid_spec,
        compiler_params=_cparams(("arbitrary", "arbitrary", "arbitrary")),
        name="moba_prompt",
    )(rel_tbl, q_arr, k_all, v_all, buckets, cache_k)


def _sample_select_kernel(n_pages, pt_ref, q_ref, ps_hbm, o_ref, ps_ref, sem):
    db = pl.program_id(0)
    nfull = n_pages // 2

    def page_copy(j):
        dst = (j % 2) * nfull + j // 2
        return pltpu.make_async_copy(ps_hbm.at[pt_ref[db * n_pages + j]], ps_ref.at[dst], sem)

    def start(j, carry):
        page_copy(j).start()
        return carry

    def wait(j, carry):
        page_copy(j).wait()
        return carry

    lax.fori_loop(0, n_pages, start, 0)
    lax.fori_loop(0, n_pages, wait, 0)

    nt = (((1,), (1,)), ((), ()))
    for h in range(MOBA_HEADS):
        means = (ps_ref[pl.ds(0, nfull), h, :] + ps_ref[pl.ds(nfull, nfull), h, :]) * (
            1.0 / MOBA_BLOCK)
        mh = means.astype(jnp.bfloat16)
        ml = (means - mh.astype(jnp.float32)).astype(jnp.bfloat16)
        q = q_ref[:, h * MOBA_HD:(h + 1) * MOBA_HD].astype(jnp.bfloat16)
        scores = (lax.dot_general(q, mh, nt, preferred_element_type=jnp.float32)
                  + lax.dot_general(q, ml, nt, preferred_element_type=jnp.float32))
        col = lax.broadcasted_iota(jnp.int32, scores.shape, 1)
        sc = scores
        out = jnp.zeros((SAMPLE_ROWS, LANES), jnp.int32)
        lane = lax.broadcasted_iota(jnp.int32, out.shape, 1)
        for r in range(MOBA_TOPK):
            m = jnp.max(sc, axis=1, keepdims=True)
            idx = jnp.min(jnp.where(sc == m, col, nfull), axis=1, keepdims=True)
            out = jnp.where(lane == r, idx, out)
            sc = jnp.where(col == idx, -jnp.inf, sc)
        o_ref[0, h] = out


def _sample_select(page_table, q_arr, q_blk0, psums):
    ndb, n_pages = page_table.shape
    grid_spec = pltpu.PrefetchScalarGridSpec(
        num_scalar_prefetch=1,
        grid=(ndb,),
        in_specs=[
            pl.BlockSpec((SAMPLE_ROWS, D_MODEL), lambda d, pt: (d, q_blk0)),
            pl.BlockSpec(memory_space=pl.ANY),
        ],
        out_specs=pl.BlockSpec((1, MOBA_HEADS, SAMPLE_ROWS, LANES), lambda d, pt: (d, 0, 0, 0)),
        scratch_shapes=[pltpu.VMEM((n_pages, MOBA_HEADS, MOBA_HD), jnp.float32),
                        pltpu.SemaphoreType.DMA],
    )
    return pl.pallas_call(
        functools.partial(_sample_select_kernel, n_pages),
        out_shape=jax.ShapeDtypeStruct((ndb, MOBA_HEADS, SAMPLE_ROWS, LANES), jnp.int32),
        grid_spec=grid_spec,
        compiler_params=_cparams(("arbitrary",)),
        name="sample_select",
    )(page_table.reshape(-1), q_arr, psums)


def _sample_attn_kernel(n_pages, t_real, layer, pt_ref, idx_ref, tbl_ref, q_ref, k_ref, v_ref,
                        bk_ref, ck_hbm, cv_hbm, o_ref, ks_ref, vs_ref, sem):
    db = pl.program_id(0)
    h = pl.program_id(1)
    ppb = MOBA_BLOCK // PAGE_SIZE
    npage = MOBA_TOPK * ppb
    step = db * MOBA_HEADS + h
    n_steps = pl.num_programs(0) * MOBA_HEADS
    slot = lax.rem(step, 2)

    def copies(st, buf, t, j):
        d = st // MOBA_HEADS
        hh = lax.rem(st, MOBA_HEADS)
        blk = idx_ref[(st * SAMPLE_ROWS + t) * MOBA_TOPK + j // ppb]
        page = pt_ref[d * n_pages + blk * ppb + j % ppb]
        rows = pl.ds(j * PAGE_SIZE, PAGE_SIZE)
        return (pltpu.make_async_copy(ck_hbm.at[layer, page, :, hh, :], ks_ref.at[buf, t, rows],
                                      sem.at[buf, 0]),
                pltpu.make_async_copy(cv_hbm.at[layer, page, :, hh, :], vs_ref.at[buf, t, rows],
                                      sem.at[buf, 1]))

    def for_all_copies(st, buf, fn):
        for t in range(t_real):
            for j in range(npage):
                ck, cv = copies(st, buf, t, j)
                fn(ck)
                fn(cv)

    @pl.when(step == 0)
    def _():
        for_all_copies(step, slot, lambda c: c.start())

    @pl.when(step + 1 < n_steps)
    def _():
        for_all_copies(step + 1, 1 - slot, lambda c: c.start())

    for_all_copies(step, slot, lambda c: c.wait())

    nt = (((1,), (1,)), ((), ()))
    bf16 = jnp.bfloat16
    own_w = 2 * SAMPLE_ROWS
    q = q_ref[...].astype(bf16)
    pad = jnp.zeros((own_w - SAMPLE_ROWS, MOBA_HD), jnp.float32)
    kn = jnp.concatenate([k_ref[...], pad], axis=0).astype(bf16)
    vn = jnp.concatenate([v_ref[...], pad], axis=0).astype(bf16)
    far_bias = tbl_ref[h * REL_BUCKETS + REL_BUCKETS - 1]
    last_blk = n_pages // ppb - 1
    r_i = lax.broadcasted_iota(jnp.int32, (SAMPLE_ROWS, own_w), 0)
    c_i = lax.broadcasted_iota(jnp.int32, (SAMPLE_ROWS, own_w), 1)
    own_bias = _bias_from_buckets(jnp.maximum(r_i - c_i, 0), tbl_ref, h)
    near = _bias_from_buckets(bk_ref[...], tbl_ref, h)
    n_key = MOBA_TOPK * MOBA_BLOCK + own_w
    row_sel = lax.broadcasted_iota(jnp.int32, (SAMPLE_ROWS, n_key), 0)
    key_col = lax.broadcasted_iota(jnp.int32, (1, n_key), 1)
    out = jnp.zeros((SAMPLE_ROWS, MOBA_HD), jnp.float32)
    out_row = lax.broadcasted_iota(jnp.int32, out.shape, 0)
    for t in range(t_real):
        keys = jnp.concatenate([ks_ref[slot, t].astype(bf16), kn], axis=0)
        vals = jnp.concatenate([vs_ref[slot, t].astype(bf16), vn], axis=0)
        bias = []
        for r in range(MOBA_TOPK):
            blk = idx_ref[(step * SAMPLE_ROWS + t) * MOBA_TOPK + r]
            bias.append(jnp.where(blk == last_blk, near[t:t + 1, :], far_bias))
        bias.append(own_bias[t:t + 1, :])
        s_all = lax.dot_general(q, keys, nt, preferred_element_type=jnp.float32)
        s_t = jnp.sum(jnp.where(row_sel == t, s_all, 0.0), axis=0, keepdims=True)
        own_col = key_col - MOBA_TOPK * MOBA_BLOCK
        visible = (own_col < 0) | ((own_col <= t) & (own_col < t_real))
        s_t = jnp.where(visible, s_t + jnp.concatenate(bias, axis=1), NEG_INF)
        m = jnp.max(s_t, axis=1, keepdims=True)
        p = jnp.exp2(s_t - m)
        den = jnp.sum(p, axis=1, keepdims=True)
        p8 = jnp.broadcast_to(p, (SAMPLE_ROWS, n_key)).astype(bf16)
        o_t = jnp.dot(p8, vals, preferred_element_type=jnp.float32) / den
        out = jnp.where(out_row == t, o_t, out)
    o_ref[...] = out.astype(o_ref.dtype)


def _sample_attn(page_table, sel_idx, rel_tbl, q_arr, q_blk0, k_all, v_all, cache_k, cache_v,
                 layer, t_real, out_dtype):
    ndb, n_pages = page_table.shape
    n = k_all.shape[1]
    past = n_pages * PAGE_SIZE
    last0 = past - MOBA_BLOCK
    t = np.arange(SAMPLE_ROWS)
    r = np.arange(MOBA_BLOCK)
    near_buckets = jnp.asarray(_rel_bucket_np(past + t[:, None] - (last0 + r[None, :])))
    grid_spec = pltpu.PrefetchScalarGridSpec(
        num_scalar_prefetch=3,
        grid=(ndb, MOBA_HEADS),
        in_specs=[
            pl.BlockSpec((SAMPLE_ROWS, MOBA_HD), lambda d, h, *_: (d, q_blk0 + h)),
            pl.BlockSpec((None, SAMPLE_ROWS, MOBA_HD), lambda d, h, *_: (layer, d, h)),
            pl.BlockSpec((None, SAMPLE_ROWS, MOBA_HD), lambda d, h, *_: (layer, d, h)),
            pl.BlockSpec((SAMPLE_ROWS, MOBA_BLOCK), lambda d, h, *_: (0, 0)),
            pl.BlockSpec(memory_space=pl.ANY),
            pl.BlockSpec(memory_space=pl.ANY),
        ],
        out_specs=pl.BlockSpec((SAMPLE_ROWS, MOBA_HD), lambda d, h, *_: (d, h)),
        scratch_shapes=[
            pltpu.VMEM((2, t_real, MOBA_TOPK * MOBA_BLOCK, MOBA_HD), jnp.float32),
            pltpu.VMEM((2, t_real, MOBA_TOPK * MOBA_BLOCK, MOBA_HD), jnp.float32),
            pltpu.SemaphoreType.DMA((2, 2)),
        ],
    )
    return pl.pallas_call(
        functools.partial(_sample_attn_kernel, n_pages, t_real, layer),
        out_shape=jax.ShapeDtypeStruct((n, D_MODEL), out_dtype),
        grid_spec=grid_spec,
        compiler_params=_cparams(("arbitrary", "arbitrary")),
        name="sample_attn",
    )(page_table.reshape(-1), sel_idx, rel_tbl, q_arr, k_all, v_all, near_buckets, cache_k,
      cache_v)


def _layer_norm(x, g, b):
    mu = jnp.mean(x, axis=-1, keepdims=True)
    xc = x - mu
    var = jnp.mean(xc * xc, axis=-1, keepdims=True)
    return xc * lax.rsqrt(var + LN_EPS) * g + b


def _mix_kernel(r_ref, m_ref, ga_ref, gb_ref, x_ref, wr_ref, wm_ref, wo_ref, g_ref, b_ref,
                wh_ref, wl_ref, x1_ref, ri_ref, rg_ref, cnt_ref, carry_ref):
    i = pl.program_id(0)

    @pl.when(i == 0)
    def _():
        carry_ref[...] = jnp.zeros_like(carry_ref)

    f32 = jnp.float32
    a = jnp.dot(r_ref[...].astype(jnp.bfloat16), wr_ref[...], preferred_element_type=f32)
    m = jnp.dot(m_ref[...].astype(jnp.bfloat16), wm_ref[...], preferred_element_type=f32)
    mix = _sigmoid(ga_ref[...].astype(f32)) * a + _sigmoid(gb_ref[...].astype(f32)) * m
    mixed = jnp.dot(mix.astype(jnp.bfloat16), wo_ref[...], preferred_element_type=f32)
    x1 = _layer_norm(DN_ALPHA * x_ref[...] + mixed, g_ref[...], b_ref[...])
    x1_ref[...] = x1

    xh = x1.astype(jnp.bfloat16)
    xl = (x1 - xh.astype(f32)).astype(jnp.bfloat16)
    logits = (jnp.dot(xh, wh_ref[...], preferred_element_type=f32)
              + jnp.dot(xl, wh_ref[...], preferred_element_type=f32)
              + jnp.dot(xh, wl_ref[...], preferred_element_type=f32))
    lane = lax.broadcasted_iota(jnp.int32, logits.shape, 1)
    is_g = lane < MOE_GROUPS
    lgm = jnp.where(is_g, logits, -jnp.inf)
    mg = jnp.max(lgm, axis=1, keepdims=True)
    grp = jnp.min(jnp.where(lgm == mg, lane, LANES), axis=1, keepdims=True)
    pg = 1.0 / jnp.sum(jnp.where(is_g, jnp.exp(logits - mg), 0.0), axis=1, keepdims=True)
    e_lane = lane - MOE_GROUPS
    in_grp = (e_lane >= 0) & (e_lane < MOE_EXPERTS) & ((e_lane >> 3) == grp)
    v1 = jnp.where(in_grp, logits, -jnp.inf)
    t1 = jnp.max(v1, axis=1, keepdims=True)
    i1 = jnp.min(jnp.where(v1 == t1, lane, LANES), axis=1, keepdims=True)
    v2 = jnp.where(lane == i1, -jnp.inf, v1)
    t2 = jnp.max(v2, axis=1, keepdims=True)
    i2 = jnp.min(jnp.where(v2 == t2, lane, LANES), axis=1, keepdims=True)
    z = jnp.exp(t2 - t1)
    g1 = pg / (1.0 + z)
    g2 = pg * z / (1.0 + z)
    oh = jnp.where((lane == i1) | (lane == i2), 1.0, 0.0)
    tm = oh.shape[0]
    tri = jnp.where(lax.broadcasted_iota(jnp.int32, (tm, tm), 0)
                    > lax.broadcasted_iota(jnp.int32, (tm, tm), 1), 1.0, 0.0)
    cum = jnp.dot(tri.astype(jnp.bfloat16), oh.astype(jnp.bfloat16),
                  preferred_element_type=f32) + carry_ref[...]
    r1 = jnp.sum(jnp.where(lane == i1, cum, 0.0), axis=1, keepdims=True)
    r2 = jnp.sum(jnp.where(lane == i2, cum, 0.0), axis=1, keepdims=True)
    carry_ref[...] = carry_ref[...] + jnp.sum(oh, axis=0, keepdims=True)
    ri = jnp.where(lane == 0, i1 - MOE_GROUPS,
                   jnp.where(lane == 1, i2 - MOE_GROUPS,
                             jnp.where(lane == 2, r1.astype(jnp.int32),
                                       jnp.where(lane == 3, r2.astype(jnp.int32), 0))))
    ri_ref[...] = ri
    rg_ref[...] = jnp.where(lane == 0, g1, jnp.where(lane == 1, g2, 0.0))
    cnt_ref[...] = carry_ref[...]


def _mix(ret_act, moba_o, gates, x, w_ret_o, w_moba_o, w_out, ln_g, ln_b, w_hi, w_lo, tm):
    n = x.shape[0]
    full = lambda shape: pl.BlockSpec(shape, lambda i: (0, 0))
    return pl.pallas_call(
        _mix_kernel,
        out_shape=(jax.ShapeDtypeStruct((n, D_MODEL), jnp.float32),
                   jax.ShapeDtypeStruct((n, LANES), jnp.int32),
                   jax.ShapeDtypeStruct((n, LANES), jnp.float32),
                   jax.ShapeDtypeStruct((1, LANES), jnp.float32)),
        grid=(n // tm,),
        in_specs=[
            pl.BlockSpec((tm, RET_W), lambda i: (i, 0)),
            pl.BlockSpec((tm, D_MODEL), lambda i: (i, 0)),
            pl.BlockSpec((tm, D_MODEL), lambda i: (i, 0)),
            pl.BlockSpec((tm, D_MODEL), lambda i: (i, 1)),
            pl.BlockSpec((tm, D_MODEL), lambda i: (i, 0)),
            full((RET_W, D_MODEL)), full((D_MODEL, D_MODEL)), full((D_MODEL, D_MODEL)),
            full((1, D_MODEL)), full((1, D_MODEL)),
            full((D_MODEL, LANES)), full((D_MODEL, LANES)),
        ],
        out_specs=(pl.BlockSpec((tm, D_MODEL), lambda i: (i, 0)),
                   pl.BlockSpec((tm, LANES), lambda i: (i, 0)),
                   pl.BlockSpec((tm, LANES), lambda i: (i, 0)),
                   pl.BlockSpec((1, LANES), lambda i: (0, 0))),
        scratch_shapes=[pltpu.VMEM((1, LANES), jnp.float32)],
        compiler_params=_cparams(("arbitrary",)),
        name="mix_ln_router",
    )(ret_act, moba_o, gates, gates, x, w_ret_o, w_moba_o, w_out, ln_g, ln_b, w_hi, w_lo)


def _load_dest(dest_hbm, dest_ref, sem):
    cp = pltpu.make_async_copy(dest_hbm.at[pl.program_id(0)], dest_ref, sem)
    cp.start()
    cp.wait()


def _dest_at(dest_ref, t, k):
    e = 2 * t + k
    return dest_ref[lax.shift_right_logical(e, 7), lax.bitwise_and(e, LANES - 1)]


def _dispatch_kernel(tm, dest_hbm, x_ref, z_hbm, o_hbm, dest_ref, sem, isem):
    del z_hbm
    _load_dest(dest_hbm, dest_ref, isem)

    def row_copy(t, k):
        return pltpu.make_async_copy(x_ref.at[pl.ds(t, 1)],
                                     o_hbm.at[pl.ds(_dest_at(dest_ref, t, k), 1)], sem)

    def start(t, carry):
        row_copy(t, 0).start()
        row_copy(t, 1).start()
        return carry

    def wait(t, carry):
        row_copy(t, 0).wait()
        row_copy(t, 1).wait()
        return carry

    lax.fori_loop(0, tm, start, 0, unroll=ROW_DMA_UNROLL)
    lax.fori_loop(0, tm, wait, 0, unroll=ROW_DMA_UNROLL)


def _dispatch(x1, dest, n_rows, tm):
    n = x1.shape[0]
    return pl.pallas_call(
        functools.partial(_dispatch_kernel, tm),
        out_shape=jax.ShapeDtypeStruct((n_rows, D_MODEL), jnp.float32),
        grid=(n // tm,),
        in_specs=[pl.BlockSpec(memory_space=pl.ANY),
                  pl.BlockSpec((tm, D_MODEL), lambda i: (i, 0)),
                  pl.BlockSpec(memory_space=pl.ANY)],
        out_specs=pl.BlockSpec(memory_space=pl.ANY),
        scratch_shapes=[pltpu.SMEM((2 * tm // LANES, LANES), jnp.int32),
                        pltpu.SemaphoreType.DMA, pltpu.SemaphoreType.DMA],
        input_output_aliases={2: 0},
        compiler_params=_cparams(("arbitrary",)),
        name="moe_dispatch",
    )(dest, x1, jnp.zeros((n_rows, D_MODEL), jnp.float32))


def _experts_kernel(be_ref, nu_ref, x_ref, wg_ref, wu_ref, wd_ref, o_ref, wgb, wub, wdb):
    i = pl.program_id(0)
    used = i < nu_ref[0]
    first = (i == 0) | (be_ref[i] != be_ref[jnp.maximum(i - 1, 0)])

    @pl.when(used & first)
    def _():
        wgb[...] = wg_ref[...].astype(jnp.bfloat16)
        wub[...] = wu_ref[...].astype(jnp.bfloat16)
        wdb[...] = wd_ref[...].astype(jnp.bfloat16)

    @pl.when(used)
    def _():
        xb = x_ref[...].astype(jnp.bfloat16)
        g = jnp.dot(xb, wgb[...], preferred_element_type=jnp.float32)
        u = jnp.dot(xb, wub[...], preferred_element_type=jnp.float32)
        hid = (g * _sigmoid(g) * u).astype(jnp.bfloat16)
        o_ref[...] = jnp.dot(hid, wdb[...], preferred_element_type=jnp.float32)

    @pl.when(jnp.logical_not(used))
    def _():
        o_ref[...] = jnp.zeros_like(o_ref)


def _experts(xs, blk_exp, n_used, w_gate, w_up, w_down, layer):
    n_rows = xs.shape[0]
    r = MOE_ROWS
    grid_spec = pltpu.PrefetchScalarGridSpec(
        num_scalar_prefetch=2,
        grid=(n_rows // r,),
        in_specs=[
            pl.BlockSpec((r, D_MODEL), lambda i, be, nu: (i, 0)),
            pl.BlockSpec((None, None, D_MODEL, MOE_FF), lambda i, be, nu: (layer, be[i], 0, 0)),
            pl.BlockSpec((None, None, D_MODEL, MOE_FF), lambda i, be, nu: (layer, be[i], 0, 0)),
            pl.BlockSpec((None, None, MOE_FF, D_MODEL), lambda i, be, nu: (layer, be[i], 0, 0)),
        ],
        out_specs=pl.BlockSpec((r, D_MODEL), lambda i, be, nu: (i, 0)),
        scratch_shapes=[pltpu.VMEM((D_MODEL, MOE_FF), jnp.bfloat16),
                        pltpu.VMEM((D_MODEL, MOE_FF), jnp.bfloat16),
                        pltpu.VMEM((MOE_FF, D_MODEL), jnp.bfloat16)],
    )
    return pl.pallas_call(
        _experts_kernel,
        out_shape=jax.ShapeDtypeStruct((n_rows, D_MODEL), jnp.float32),
        grid_spec=grid_spec,
        compiler_params=_cparams(("arbitrary",)),
        name="moe_experts",
    )(blk_exp, n_used, xs, w_gate, w_up, w_down)


def _combine_kernel(tm, dest_hbm, x_ref, gt_ref, g_ref, b_ref, y_hbm, o_ref, ob_ref,
                    ya_ref, yb_ref, dest_ref, sem, isem):
    _load_dest(dest_hbm, dest_ref, isem)

    def row_copy(t, k):
        buf = ya_ref if k == 0 else yb_ref
        return pltpu.make_async_copy(y_hbm.at[pl.ds(_dest_at(dest_ref, t, k), 1)],
                                     buf.at[pl.ds(t, 1)], sem.at[k])

    def start(t, carry):
        row_copy(t, 0).start()
        row_copy(t, 1).start()
        return carry

    def wait(t, carry):
        row_copy(t, 0).wait()
        row_copy(t, 1).wait()
        return carry

    lax.fori_loop(0, tm, start, 0, unroll=ROW_DMA_UNROLL)
    lax.fori_loop(0, tm, wait, 0, unroll=ROW_DMA_UNROLL)
    gt = gt_ref[...]
    y = gt[:, 0:1] * ya_ref[...] + gt[:, 1:2] * yb_ref[...]
    x2 = _layer_norm(DN_ALPHA * x_ref[...] + y, g_ref[...], b_ref[...])
    o_ref[...] = x2
    ob_ref[...] = x2.astype(jnp.bfloat16)


def _combine(x1, gates, dest, ys, ln_g, ln_b, tm):
    n = x1.shape[0]
    return pl.pallas_call(
        functools.partial(_combine_kernel, tm),
        out_shape=(jax.ShapeDtypeStruct((n, D_MODEL), jnp.float32),
                   jax.ShapeDtypeStruct((n, D_MODEL), jnp.bfloat16)),
        grid=(n // tm,),
        in_specs=[pl.BlockSpec(memory_space=pl.ANY),
                  pl.BlockSpec((tm, D_MODEL), lambda i: (i, 0)),
                  pl.BlockSpec((tm, LANES), lambda i: (i, 0)),
                  pl.BlockSpec((1, D_MODEL), lambda i: (0, 0)),
                  pl.BlockSpec((1, D_MODEL), lambda i: (0, 0)),
                  pl.BlockSpec(memory_space=pl.ANY)],
        out_specs=(pl.BlockSpec((tm, D_MODEL), lambda i: (i, 0)),
                   pl.BlockSpec((tm, D_MODEL), lambda i: (i, 0))),
        scratch_shapes=[pltpu.VMEM((tm, D_MODEL), jnp.float32),
                        pltpu.VMEM((tm, D_MODEL), jnp.float32),
                        pltpu.SMEM((2 * tm // LANES, LANES), jnp.int32),
                        pltpu.SemaphoreType.DMA((2,)), pltpu.SemaphoreType.DMA],
        compiler_params=_cparams(("arbitrary",)),
        name="moe_combine_ln",
    )(dest, x1, gates, ln_g, ln_b, ys)


def _moe(x1, route_i, route_g, counts, w_gate, w_up, w_down, layer, ln_g, ln_b, tm):
    n = x1.shape[0]
    r = MOE_ROWS
    n_blk = (2 * n) // r + MOE_EXPERTS
    cnt = counts[0, MOE_GROUPS:MOE_GROUPS + MOE_EXPERTS].astype(jnp.int32)
    padded = (cnt + r - 1) // r * r
    pad_end = jnp.cumsum(padded)
    pad_start = pad_end - padded
    dest = (pad_start[route_i[:, 0:2]] + route_i[:, 2:4]).astype(jnp.int32)
    dest = dest.reshape(n // tm, 2 * tm // LANES, LANES)
    blk_start = jnp.arange(n_blk, dtype=jnp.int32) * r
    blk_exp = jnp.minimum(jnp.sum(pad_end[None, :] <= blk_start[:, None], axis=1),
                          MOE_EXPERTS - 1).astype(jnp.int32)
    n_used = (pad_end[-1:] // r).astype(jnp.int32)
    xs = _dispatch(x1, dest, n_blk * r, tm)
    ys = _experts(xs, blk_exp, n_used, w_gate, w_up, w_down, layer)
    return _combine(x1, route_g, dest, ys, ln_g, ln_b, tm)


def _split_hi_lo(w):
    hi = w.astype(jnp.bfloat16)
    return hi, (w - hi.astype(jnp.float32)).astype(jnp.bfloat16)


def _layer_common(x, xb, k_prev, v_prev, attn_fn, lw, layer, tm_proj, tm, tm_moe, act_dtype):
    (w_in, col_scale, w_ret_o, w_moba_o, w_out, ln1_g, ln1_b, w_hi, w_lo,
     w_gate, w_up, w_down, ln2_g, ln2_b) = lw
    proj = functools.partial(_proj, xb, w_in, layer, col_scale)
    qk = proj(COL_RQ, 2, jnp.float32, tm_proj)
    vgq = proj(COL_RV, 5, act_dtype, tm_proj)
    k_all = _proj_into(xb, w_in, layer, col_scale, COL_MK, k_prev, tm_proj)
    v_all = _proj_into(xb, w_in, layer, col_scale, COL_MK + 1, v_prev, tm_proj)
    gates = proj(COL_GA, 2, act_dtype, tm_proj)
    ret_act, ret_state, moba_o = attn_fn(qk, vgq, k_all, v_all)
    x1, route_i, route_g, counts = _mix(ret_act, moba_o, gates, x, w_ret_o, w_moba_o, w_out,
                                        ln1_g, ln1_b, w_hi, w_lo, tm)
    x2, x2b = _moe(x1, route_i, route_g, counts, w_gate, w_up, w_down, layer, ln2_g, ln2_b,
                   tm_moe)
    return x2, x2b, k_all, v_all, ret_state


def kernel(x_prompt, x_sample, cache_k, cache_v, state_ret, page_table, rel_bias, w_in, ret_gn_w,
           w_ret_o, w_moba_o, w_out, ln1_g, ln1_b, w_group, w_router, w_exp_gate, w_exp_up,
           w_exp_down, ln2_g, ln2_b):
    b, s, _ = x_prompt.shape
    db, t_real, _ = x_sample.shape
    n_pages = page_table.shape[1]
    past = n_pages * PAGE_SIZE
    bf16 = jnp.bfloat16

    col_scale = np.ones((1, PROJ_TILES * D_MODEL), np.float32)
    col_scale[0, D_MODEL:2 * D_MODEL] = RET_DK ** -0.5
    col_scale[0, COL_MQ * D_MODEL:(COL_MQ + 1) * D_MODEL] = MOBA_HD ** -0.5 * LOG2E
    col_scale = jnp.asarray(col_scale)
    rel_tbl = (rel_bias.T * LOG2E).reshape(-1)

    cos_p, sin_p = _xpos_tables(jnp.arange(s))
    pos_s = jnp.minimum(past + jnp.arange(SAMPLE_ROWS), past + t_real - 1)
    cos_s, sin_s = _xpos_tables(pos_s)

    xp = x_prompt.reshape(b * s, D_MODEL)
    xs = jnp.pad(x_sample, ((0, 0), (0, SAMPLE_ROWS - t_real), (0, 0))).reshape(
        db * SAMPLE_ROWS, D_MODEL)
    xpb, xsb = xp.astype(bf16), xs.astype(bf16)
    zero_state = jnp.zeros((b, RET_HEADS, RET_DK, RET_DV), jnp.float32)
    state_all = state_ret.reshape(DEPTH * db, RET_HEADS, RET_DK, RET_DV)
    mq_blk = (COL_MQ - COL_RV) * D_MODEL // MOBA_HD

    rp = []
    kp, vp = (jnp.zeros((DEPTH, b * s, D_MODEL), jnp.float32) for _ in range(2))
    ks, vs = (jnp.zeros((DEPTH, db * SAMPLE_ROWS, D_MODEL), jnp.float32) for _ in range(2))
    st_s = jnp.zeros((DEPTH * db, RET_HEADS, RET_DK, RET_DV), jnp.float32)
    for l in range(DEPTH):
        w_gr = jnp.zeros((D_MODEL, LANES), jnp.float32)
        w_gr = w_gr.at[:, :MOE_GROUPS].set(w_group[l])
        w_gr = w_gr.at[:, MOE_GROUPS:MOE_GROUPS + MOE_EXPERTS].set(w_router[l])
        w_hi, w_lo = _split_hi_lo(w_gr)
        lw = (w_in, col_scale, w_ret_o[l].astype(bf16), w_moba_o[l].astype(bf16),
              w_out[l].astype(bf16), ln1_g[l][None], ln1_b[l][None], w_hi, w_lo,
              w_exp_gate, w_exp_up, w_exp_down, ln2_g[l][None], ln2_b[l][None])
        gn = ret_gn_w[l][None]

        page_sums = []

        def prompt_attn(qk, vgq, k_all, v_all, l=l, page_sums=page_sums):
            ret_act, st = _retention(qk, vgq, 0, 1, cos_p, sin_p, gn, zero_state, 0,
                                     b, s // RET_CHUNK, RET_CHUNK, RET_CHUNK, bf16)
            moba_o, psums = _moba_prompt(vgq, mq_blk, k_all, v_all, l, rel_tbl, cache_k, b, s,
                                         bf16)
            page_sums.append(psums)
            return ret_act, st, moba_o

        def sample_attn(qk, vgq, k_all, v_all, l=l, st_prev=st_s, page_sums=page_sums):
            ret_act, st = _retention(qk, vgq, 0, 1, cos_s, sin_s, gn, state_all, l * db,
                                     db, 1, SAMPLE_ROWS, t_real, jnp.float32,
                                     sf_rows=DEPTH * db, sf_off=l * db, sf_prev=st_prev)
            sel = _sample_select(page_table, vgq, COL_MQ - COL_RV, page_sums[0])
            sel_idx = sel[:, :, :, :MOBA_TOPK].reshape(-1)
            moba_o = _sample_attn(page_table, sel_idx, rel_tbl, vgq, mq_blk, k_all, v_all,
                                  cache_k, cache_v, l, t_real, jnp.float32)
            return ret_act, st, moba_o

        n_s = db * SAMPLE_ROWS
        xp, xpb, kp, vp, st_p = _layer_common(xp, xpb, kp, vp, prompt_attn, lw, l,
                                              min(1024, b * s), 256, 512, bf16)
        tm_s = min(256, n_s)
        xs, xsb, ks, vs, st_s = _layer_common(xs, xsb, ks, vs, sample_attn, lw, l, tm_s, tm_s,
                                              tm_s, jnp.float32)
        rp.append(st_p)

    page_shape = (DEPTH, b, s // PAGE_SIZE, PAGE_SIZE, MOBA_HEADS, MOBA_HD)
    new_shape = (DEPTH, db, SAMPLE_ROWS, MOBA_HEADS, MOBA_HD)
    yp = xp.reshape(b, s, D_MODEL)
    ys = xs.reshape(db, SAMPLE_ROWS, D_MODEL)[:, :t_real]
    return (yp, ys, kp.reshape(page_shape), vp.reshape(page_shape), jnp.stack(rp),
            ks.reshape(new_shape)[:, :, :t_real], vs.reshape(new_shape)[:, :, :t_real],
            st_s.reshape(DEPTH, db, RET_HEADS, RET_DK, RET_DV))
```

```python
import functools
import math

import numpy as np
import jax
import jax.numpy as jnp
from jax import lax
from jax.experimental import pallas as pl
from jax.experimental.pallas import tpu as pltpu

D_MODEL = 1024
DEPTH = 2
PAGE_SIZE = 128
RET_HEADS = 4
RET_DK = D_MODEL // RET_HEADS
RET_DV = 2 * D_MODEL // RET_HEADS
RET_W = RET_HEADS * RET_DV
XPOS_BASE = 10000.0
GN_EPS = 1e-6
MOBA_HEADS = 8
MOBA_HD = D_MODEL // MOBA_HEADS
MOBA_BLOCK = 256
MOBA_TOPK = 3
REL_BUCKETS = 32
REL_MAX_DIST = 128
MOE_GROUPS = 4
MOE_EPG = 8
MOE_EXPERTS = MOE_GROUPS * MOE_EPG
MOE_FF = D_MODEL // 2
DN_ALPHA = (2.0 * DEPTH) ** 0.25
LN_EPS = 1e-5
NEG_INF = -1e30
LOG2E = math.log2(math.e)

COL_RQ, COL_RV, COL_RG, COL_MQ, COL_MK, COL_GA = 0, 2, 4, 6, 7, 9
PROJ_TILES = 11

LANES = 128
SUBLANES = 8
VMEM_LIMIT = 56 * 1024 * 1024
SAMPLE_ROWS = 8
RET_STEP_ROWS = 256
MOE_ROWS = 256
ROW_DMA_UNROLL = 8
MOBA_ONES_ROWS = 16
MOBA_GROUP = 4


def _cparams(sem):
    return pltpu.CompilerParams(dimension_semantics=sem, vmem_limit_bytes=VMEM_LIMIT)


def _sigmoid(x):
    return 1.0 / (1.0 + jnp.exp(-x))


def _proj_kernel(x_ref, w_ref, s_ref, o_ref, wb_ref):
    @pl.when(pl.program_id(1) == 0)
    def _():
        wb_ref[...] = w_ref[...].astype(jnp.bfloat16)

    acc = jnp.dot(x_ref[...], wb_ref[...], preferred_element_type=jnp.float32)
    o_ref[...] = (acc * s_ref[...]).astype(o_ref.dtype)


def _proj(xb, w_in, layer, col_scale, col0, ncols, out_dtype, tm):
    n = xb.shape[0]
    tn = D_MODEL
    return pl.pallas_call(
        _proj_kernel,
        out_shape=jax.ShapeDtypeStruct((n, ncols * tn), out_dtype),
        grid=(ncols, n // tm),
        in_specs=[
            pl.BlockSpec((tm, D_MODEL), lambda j, i: (i, 0)),
            pl.BlockSpec((None, D_MODEL, tn), lambda j, i: (layer, 0, j + col0)),
            pl.BlockSpec((1, tn), lambda j, i: (0, j + col0)),
        ],
        out_specs=pl.BlockSpec((tm, tn), lambda j, i: (i, j)),
        scratch_shapes=[pltpu.VMEM((D_MODEL, tn), jnp.bfloat16)],
        compiler_params=_cparams(("arbitrary", "arbitrary")),
        name="in_proj",
    )(xb, w_in, col_scale)


def _proj_layer_kernel(x_ref, w_ref, s_ref, prev_ref, o_ref, wb_ref):
    del prev_ref
    _proj_kernel(x_ref, w_ref, s_ref, o_ref, wb_ref)


def _proj_into(xb, w_in, layer, col_scale, col, prev, tm):
    n = xb.shape[0]
    tn = D_MODEL
    return pl.pallas_call(
        _proj_layer_kernel,
        out_shape=jax.ShapeDtypeStruct((DEPTH, n, tn), jnp.float32),
        grid=(1, n // tm),
        in_specs=[
            pl.BlockSpec((tm, D_MODEL), lambda j, i: (i, 0)),
            pl.BlockSpec((None, D_MODEL, tn), lambda j, i: (layer, 0, col)),
            pl.BlockSpec((1, tn), lambda j, i: (0, col)),
            pl.BlockSpec(memory_space=pl.ANY),
        ],
        out_specs=pl.BlockSpec((None, tm, tn), lambda j, i: (layer, i, 0)),
        scratch_shapes=[pltpu.VMEM((D_MODEL, tn), jnp.bfloat16)],
        input_output_aliases={3: 0},
        compiler_params=_cparams(("arbitrary", "arbitrary")),
        name="in_proj_kv",
    )(xb, w_in, col_scale, prev)


def _ret_consts(c, c_true):
    lg = np.log1p(-np.exp2(-5.0 - np.arange(RET_HEADS, dtype=np.float64)))
    j = np.arange(c, dtype=np.float64)
    diff = j[:, None] - j[None, :]
    dmask = np.where(diff >= 0, np.exp(lg[:, None, None] * np.maximum(diff, 0.0)), 0.0)
    qdec = np.exp(lg[:, None] * (j + 1.0))[..., None]
    kdec = np.where(j < c_true, np.exp(lg[:, None] * (c_true - 1.0 - j)), 0.0)[..., None]
    sdec = [float(v) for v in np.exp(lg * c_true)]
    return (jnp.asarray(dmask, jnp.float32), jnp.asarray(qdec, jnp.float32),
            jnp.asarray(kdec, jnp.float32), sdec)


def _ret_kernel(sdec, q_ref, k_ref, v_ref, g_ref, cos_ref, sin_ref, dm_ref, qd_ref, kd_ref,
                gn_ref, s0_ref, o_ref, sf_ref, st_ref):
    c = pl.program_id(1)

    @pl.when(c == 0)
    def _():
        st_ref[...] = s0_ref[0]

    cos = cos_ref[...]
    sin = sin_ref[...]
    half = RET_DK // 2

    def rot(x):
        x1, x2 = x[:, :half], x[:, half:]
        return jnp.concatenate([x1 * cos - x2 * sin, x1 * sin + x2 * cos], axis=-1)

    for h in range(RET_HEADS):
        qr = rot(q_ref[:, h * RET_DK:(h + 1) * RET_DK])
        kr = rot(k_ref[:, h * RET_DK:(h + 1) * RET_DK])
        qb = qr.astype(jnp.bfloat16)
        kb = kr.astype(jnp.bfloat16)
        v = v_ref[:, h * RET_DV:(h + 1) * RET_DV].astype(jnp.bfloat16)
        st = st_ref[h]
        s = lax.dot_general(qb, kb, (((1,), (1,)), ((), ())),
                            preferred_element_type=jnp.float32) * dm_ref[h]
        inner = jnp.dot(s.astype(jnp.bfloat16), v, preferred_element_type=jnp.float32)
        cross = jnp.dot(qb, st.astype(jnp.bfloat16),
                        preferred_element_type=jnp.float32) * qd_ref[h]
        kdb = (kr * kd_ref[h]).astype(jnp.bfloat16)
        st_ref[h] = sdec[h] * st + lax.dot_general(
            kdb, v, (((0,), (0,)), ((), ())), preferred_element_type=jnp.float32)
        o = inner + cross
        mu = jnp.mean(o, axis=-1, keepdims=True)
        oc = o - mu
        var = jnp.mean(oc * oc, axis=-1, keepdims=True)
        on = oc * lax.rsqrt(var + GN_EPS) * gn_ref[:, h * RET_DV:(h + 1) * RET_DV]
        g = g_ref[:, h * RET_DV:(h + 1) * RET_DV].astype(jnp.float32)
        o_ref[:, h * RET_DV:(h + 1) * RET_DV] = (on * (g * _sigmoid(g))).astype(o_ref.dtype)

    @pl.when(c == pl.num_programs(1) - 1)
    def _():
        sf_ref[0] = st_ref[...]


def _ret_layer_kernel(sdec, *refs):
    _ret_kernel(sdec, *refs[:11], *refs[12:])


def _retention(qk, vg, v_blk, g_blk, cos, sin, gn_w, state0, s0_off, nb, nc, c, c_true,
               out_dtype, sf_rows=None, sf_off=0, sf_prev=None):
    n = qk.shape[0]
    sf_rows = nb if sf_rows is None else sf_rows
    dmask, qdec, kdec, sdec = _ret_consts(c, c_true)
    row = lambda b, i: b * nc + i
    extra_specs = [] if sf_prev is None else [pl.BlockSpec(memory_space=pl.ANY)]
    extra_args = [] if sf_prev is None else [sf_prev]
    return pl.pallas_call(
        functools.partial(_ret_kernel if sf_prev is None else _ret_layer_kernel, sdec),
        out_shape=(jax.ShapeDtypeStruct((n, RET_W), out_dtype),
                   jax.ShapeDtypeStruct((sf_rows, RET_HEADS, RET_DK, RET_DV), jnp.float32)),
        grid=(nb, nc),
        input_output_aliases={} if sf_prev is None else {11: 1},
        in_specs=[
            pl.BlockSpec((c, D_MODEL), lambda b, i: (row(b, i), 0)),
            pl.BlockSpec((c, D_MODEL), lambda b, i: (row(b, i), 1)),
            pl.BlockSpec((c, RET_W), lambda b, i: (row(b, i), v_blk)),
            pl.BlockSpec((c, RET_W), lambda b, i: (row(b, i), g_blk)),
            pl.BlockSpec((c, RET_DK // 2), lambda b, i: (i, 0)),
            pl.BlockSpec((c, RET_DK // 2), lambda b, i: (i, 0)),
            pl.BlockSpec((RET_HEADS, c, c), lambda b, i: (0, 0, 0)),
            pl.BlockSpec((RET_HEADS, c, 1), lambda b, i: (0, 0, 0)),
            pl.BlockSpec((RET_HEADS, c, 1), lambda b, i: (0, 0, 0)),
            pl.BlockSpec((1, RET_W), lambda b, i: (0, 0)),
            pl.BlockSpec((1, RET_HEADS, RET_DK, RET_DV), lambda b, i: (s0_off + b, 0, 0, 0)),
        ] + extra_specs,
        out_specs=(pl.BlockSpec((c, RET_W), lambda b, i: (row(b, i), 0)),
                   pl.BlockSpec((1, RET_HEADS, RET_DK, RET_DV),
                                lambda b, i: (sf_off + b, 0, 0, 0))),
        scratch_shapes=[pltpu.VMEM((RET_HEADS, RET_DK, RET_DV), jnp.float32)],
        compiler_params=_cparams(("arbitrary", "arbitrary")),
        name="retention",
    )(qk, qk, vg, vg, cos, sin, dmask, qdec, kdec, gn_w, state0, *extra_args)


def _xpos_tables(pos):
    half = RET_DK // 2
    inv = 1.0 / (XPOS_BASE ** jnp.linspace(0.0, 1.0, half, dtype=jnp.float32))
    ang = pos.astype(jnp.float32)[:, None] * inv[None, :]
    return jnp.cos(ang), jnp.sin(ang)


def _rel_bucket_np(rel):
    n = np.maximum(rel, 0)
    exact = REL_BUCKETS // 2
    nf = np.maximum(n, 1).astype(np.float32)
    large = exact + (np.log(nf / np.float32(exact)) / np.float32(math.log(REL_MAX_DIST / exact))
                     * np.float32(REL_BUCKETS - exact)).astype(np.int32)
    return np.where(n < exact, n, np.minimum(large, REL_BUCKETS - 1)).astype(np.int32)


def _bias_from_buckets(bucket, table_ref, h):
    out = jnp.zeros(bucket.shape, jnp.float32)
    for b in range(REL_BUCKETS):
        out = jnp.where(bucket == b, table_ref[h * REL_BUCKETS + b], out)
    return out


def _top3_mask(scores, valid, row, nrow):
    sc = jnp.where(valid, scores, NEG_INF)
    sel = jnp.zeros(scores.shape, jnp.float32)
    for _ in range(MOBA_TOPK):
        m = jnp.max(sc, axis=0, keepdims=True)
        idx = jnp.min(jnp.where(sc == m, row, nrow), axis=0, keepdims=True)
        pick = row == idx
        sel = jnp.where(pick, 1.0, sel)
        sc = jnp.where(pick, -jnp.inf, sc)
    return jnp.where(valid, sel, 0.0)


def _moba_prompt_kernel(nblk, grp, tbl_ref, q_ref, k_ref, v_ref, bk_ref, c_ref, o_ref, ps_ref,
                        kb_ref, vt_ref, mh_ref, ml_ref, bown_ref, bprev_ref, sel_ref, acc_ref,
                        sa_ref, sb_ref):
    h = pl.program_id(1)
    qi = pl.program_id(2)
    blk = MOBA_BLOCK
    bf16 = jnp.bfloat16
    f32 = jnp.float32
    nt = (((1,), (1,)), ((), ()))

    @pl.when(qi == 0)
    def _():
        kf = k_ref[...]
        kb_ref[...] = kf.astype(bf16)
        means = jnp.mean(kf.reshape(nblk, blk, MOBA_HD), axis=1)
        mh = means.astype(bf16)
        mh_ref[...] = mh
        ml_ref[...] = (means - mh.astype(f32)).astype(bf16)

        def transpose_block(j, carry):
            r0 = pl.multiple_of(j * blk, blk)
            vt_ref[j, :MOBA_HD, :] = v_ref[pl.ds(r0, blk), :].T.astype(bf16)
            vt_ref[j, MOBA_HD:, :] = jnp.ones((MOBA_ONES_ROWS, blk), bf16)
            return carry

        lax.fori_loop(0, nblk, transpose_block, 0)
        far = tbl_ref[h * REL_BUCKETS + REL_BUCKETS - 1]
        bown_ref[...] = _bias_from_buckets(bk_ref[0], tbl_ref, h) - far
        bprev_ref[...] = _bias_from_buckets(bk_ref[1], tbl_ref, h) - far

    ps_ref[...] = jnp.sum(c_ref[...], axis=1)

    q = q_ref[...]
    tq = q.shape[0]
    scores = (lax.dot_general(mh_ref[...], q, nt, preferred_element_type=f32)
              + lax.dot_general(ml_ref[...], q, nt, preferred_element_type=f32))
    row = lax.broadcasted_iota(jnp.int32, scores.shape, 0)
    sel_ref[...] = _top3_mask(scores, row < qi, row, nblk)

    def softmax_step(tiles, m_old):
        m_new = m_old
        for s, keep, _ in tiles:
            m_new = jnp.maximum(m_new, jnp.where(keep, jnp.max(s, axis=0, keepdims=True), NEG_INF))
        upd = None
        for s, keep, vt in tiles:
            p = jnp.exp2(s - jnp.where(keep, m_new, -NEG_INF))
            d = jnp.dot(vt, p.astype(bf16), preferred_element_type=f32)
            upd = d if upd is None else upd + d
        acc_ref[...] = jnp.exp2(m_old - m_new) * acc_ref[...] + upd
        return m_new

    def picked(n, ok):
        return jnp.where(ok, sel_ref[pl.ds(n, 1), :], 0.0) > 0.0

    def group_base(gi):
        return jnp.minimum(gi * grp, nblk - grp)

    def group_logits(gi):
        r0 = pl.multiple_of(group_base(gi) * blk, blk)
        return lax.dot_general(kb_ref[pl.ds(r0, grp * blk), :], q, nt, preferred_element_type=f32)

    sa_ref[...] = group_logits(0)

    acc_ref[...] = jnp.zeros_like(acc_ref)
    prev = jnp.maximum(qi - 1, 0)
    s_prev = lax.dot_general(kb_ref[pl.ds(pl.multiple_of(prev * blk, blk), blk), :], q, nt,
                             preferred_element_type=f32) + bprev_ref[...]
    s_own = lax.dot_general(kb_ref[pl.ds(pl.multiple_of(qi * blk, blk), blk), :], q, nt,
                            preferred_element_type=f32) + bown_ref[...]
    k_i = lax.broadcasted_iota(jnp.int32, s_own.shape, 0)
    q_i = lax.broadcasted_iota(jnp.int32, s_own.shape, 1)
    s_own = jnp.where(q_i >= k_i, s_own, NEG_INF)
    everyone = jnp.full((1, tq), True)
    m = softmax_step([(s_prev, picked(prev, qi >= 1), vt_ref[prev]),
                      (s_own, everyone, vt_ref[qi])], jnp.full((1, tq), NEG_INF, f32))

    n_far = jnp.maximum(qi - 1, 0)

    def far_group(gi, s_ref, m_old):
        first = gi * grp
        base = group_base(gi)
        tiles = []
        for i in range(grp):
            n = base + i
            tiles.append((s_ref[i * blk:(i + 1) * blk, :],
                          picked(n, (n >= first) & (n < n_far)), vt_ref[n]))
        return softmax_step(tiles, m_old)

    def far_pair(j, m_old):
        sb_ref[...] = group_logits(2 * j + 1)
        m_mid = far_group(2 * j, sa_ref, m_old)
        sa_ref[...] = group_logits(2 * j + 2)
        return far_group(2 * j + 1, sb_ref, m_mid)

    n_grp = (n_far + grp - 1) // grp
    lax.fori_loop(0, (n_grp + 1) // 2, far_pair, m)
    acc = acc_ref[...]
    o_ref[...] = (acc[:MOBA_HD] / acc[MOBA_HD:MOBA_HD + 1]).T.astype(o_ref.dtype)


def _moba_prompt(q_arr, q_blk0, k_all, v_all, layer, rel_tbl, cache_k, nb, s, out_dtype):
    n = k_all.shape[1]
    blk = MOBA_BLOCK
    nblk = s // blk
    grp = min(MOBA_GROUP, nblk)
    n_phys = cache_k.shape[1]
    ppstep = n_phys // (nb * MOBA_HEADS * nblk)
    assert ppstep * nb * MOBA_HEADS * nblk == n_phys, "cache pages must split evenly over steps"
    step = lambda b, h, qi: (b * MOBA_HEADS + h) * nblk + qi
    i = np.arange(blk)
    buckets = jnp.asarray(np.stack([_rel_bucket_np(i[None, :] - i[:, None]),
                                    _rel_bucket_np(blk + i[None, :] - i[:, None])]))
    grid_spec = pltpu.PrefetchScalarGridSpec(
        num_scalar_prefetch=1,
        grid=(nb, MOBA_HEADS, nblk),
        in_specs=[
            pl.BlockSpec((blk, MOBA_HD), lambda b, h, qi, t: (b * nblk + qi, q_blk0 + h)),
            pl.BlockSpec((None, s, MOBA_HD), lambda b, h, qi, t: (layer, b, h)),
            pl.BlockSpec((None, s, MOBA_HD), lambda b, h, qi, t: (layer, b, h)),
            pl.BlockSpec((2, blk, blk), lambda b, h, qi, t: (0, 0, 0)),
            pl.BlockSpec((None, ppstep, PAGE_SIZE, MOBA_HEADS, MOBA_HD),
                         lambda b, h, qi, t: (layer, step(b, h, qi), 0, 0, 0)),
        ],
        out_specs=(pl.BlockSpec((blk, MOBA_HD), lambda b, h, qi, t: (b * nblk + qi, h)),
                   pl.BlockSpec((ppstep, MOBA_HEADS, MOBA_HD),
                                lambda b, h, qi, t: (step(b, h, qi), 0, 0))),
        scratch_shapes=[
            pltpu.VMEM((s, MOBA_HD), jnp.bfloat16),
            pltpu.VMEM((nblk, MOBA_HD + MOBA_ONES_ROWS, blk), jnp.bfloat16),
            pltpu.VMEM((nblk, MOBA_HD), jnp.bfloat16),
            pltpu.VMEM((nblk, MOBA_HD), jnp.bfloat16),
            pltpu.VMEM((blk, blk), jnp.float32),
            pltpu.VMEM((blk, blk), jnp.float32),
            pltpu.VMEM((nblk, blk), jnp.float32),
            pltpu.VMEM((MOBA_HD + MOBA_ONES_ROWS, blk), jnp.float32),
            pltpu.VMEM((grp * blk, blk), jnp.float32),
            pltpu.VMEM((grp * blk, blk), jnp.float32),
        ],
    )
    return pl.pallas_call(
        functools.partial(_moba_prompt_kernel, nblk, grp),
        out_shape=(jax.ShapeDtypeStruct((n, D_MODEL), out_dtype),
                   jax.ShapeDtypeStruct((n_phys, MOBA_HEADS, MOBA_HD), jnp.float32)),
        grid_spec=grid_spec,
        compiler_params=_cparams(("arbitrary", "arbitrary", "arbitrary")),
        name="moba_prompt",
    )(rel_tbl, q_arr, k_all, v_all, buckets, cache_k)


def _sample_select_kernel(n_pages, pt_ref, q_ref, ps_hbm, o_ref, ps_ref, sem):
    db = pl.program_id(0)
    nfull = n_pages // 2

    slot = lax.rem(db, 2)

    def page_copy(d, buf, j):
        dst = (j % 2) * nfull + j // 2
        return pltpu.make_async_copy(ps_hbm.at[pt_ref[d * n_pages + j]], ps_ref.at[buf, dst],
                                     sem.at[buf])

    def start_all(d, buf):
        def body(j, carry):
            page_copy(d, buf, j).start()
            return carry
        lax.fori_loop(0, n_pages, body, 0, unroll=ROW_DMA_UNROLL)

    def wait_all(d, buf):
        def body(j, carry):
            page_copy(d, buf, j).wait()
            return carry
        lax.fori_loop(0, n_pages, body, 0, unroll=ROW_DMA_UNROLL)

    @pl.when(db == 0)
    def _():
        start_all(db, slot)

    @pl.when(db + 1 < pl.num_programs(0))
    def _():
        start_all(db + 1, 1 - slot)

    wait_all(db, slot)

    nt = (((1,), (1,)), ((), ()))
    for h in range(MOBA_HEADS):
        means = (ps_ref[slot, pl.ds(0, nfull), h, :]
                 + ps_ref[slot, pl.ds(nfull, nfull), h, :]) * (1.0 / MOBA_BLOCK)
        mh = means.astype(jnp.bfloat16)
        ml = (means - mh.astype(jnp.float32)).astype(jnp.bfloat16)
        q = q_ref[:, h * MOBA_HD:(h + 1) * MOBA_HD].astype(jnp.bfloat16)
        scores = (lax.dot_general(q, mh, nt, preferred_element_type=jnp.float32)
                  + lax.dot_general(q, ml, nt, preferred_element_type=jnp.float32))
        col = lax.broadcasted_iota(jnp.int32, scores.shape, 1)
        sc = scores
        out = jnp.zeros((SAMPLE_ROWS, LANES), jnp.int32)
        lane = lax.broadcasted_iota(jnp.int32, out.shape, 1)
        for r in range(MOBA_TOPK):
            m = jnp.max(sc, axis=1, keepdims=True)
            idx = jnp.min(jnp.where(sc == m, col, nfull), axis=1, keepdims=True)
            out = jnp.where(lane == r, idx, out)
            sc = jnp.where(col == idx, -jnp.inf, sc)
        o_ref[0, h] = out


def _sample_select(page_table, q_arr, q_blk0, psums):
    ndb, n_pages = page_table.shape
    grid_spec = pltpu.PrefetchScalarGridSpec(
        num_scalar_prefetch=1,
        grid=(ndb,),
        in_specs=[
            pl.BlockSpec((SAMPLE_ROWS, D_MODEL), lambda d, pt: (d, q_blk0)),
            pl.BlockSpec(memory_space=pl.ANY),
        ],
        out_specs=pl.BlockSpec((1, MOBA_HEADS, SAMPLE_ROWS, LANES), lambda d, pt: (d, 0, 0, 0)),
        scratch_shapes=[pltpu.VMEM((2, n_pages, MOBA_HEADS, MOBA_HD), jnp.float32),
                        pltpu.SemaphoreType.DMA((2,))],
    )
    return pl.pallas_call(
        functools.partial(_sample_select_kernel, n_pages),
        out_shape=jax.ShapeDtypeStruct((ndb, MOBA_HEADS, SAMPLE_ROWS, LANES), jnp.int32),
        grid_spec=grid_spec,
        compiler_params=_cparams(("arbitrary",)),
        name="sample_select",
    )(page_table.reshape(-1), q_arr, psums)


def _sample_attn_kernel(n_pages, t_real, layer, pt_ref, idx_ref, tbl_ref, q_ref, k_ref, v_ref,
                        bk_ref, ck_hbm, cv_hbm, o_ref, ks_ref, vs_ref, sem):
    db = pl.program_id(0)
    h = pl.program_id(1)
    ppb = MOBA_BLOCK // PAGE_SIZE
    npage = MOBA_TOPK * ppb
    step = db * MOBA_HEADS + h
    n_steps = pl.num_programs(0) * MOBA_HEADS
    slot = lax.rem(step, 2)

    def copies(st, buf, t, j):
        d = st // MOBA_HEADS
        hh = lax.rem(st, MOBA_HEADS)
        blk = idx_ref[(st * SAMPLE_ROWS + t) * MOBA_TOPK + j // ppb]
        page = pt_ref[d * n_pages + blk * ppb + j % ppb]
        rows = pl.ds(j * PAGE_SIZE, PAGE_SIZE)
        return (pltpu.make_async_copy(ck_hbm.at[layer, page, :, hh, :], ks_ref.at[buf, t, rows],
                                      sem.at[buf, 0]),
                pltpu.make_async_copy(cv_hbm.at[layer, page, :, hh, :], vs_ref.at[buf, t, rows],
                                      sem.at[buf, 1]))

    def for_all_copies(st, buf, fn):
        for t in range(t_real):
            for j in range(npage):
                ck, cv = copies(st, buf, t, j)
                fn(ck)
                fn(cv)

    @pl.when(step == 0)
    def _():
        for_all_copies(step, slot, lambda c: c.start())

    @pl.when(step + 1 < n_steps)
    def _():
        for_all_copies(step + 1, 1 - slot, lambda c: c.start())

    for_all_copies(step, slot, lambda c: c.wait())

    nt = (((1,), (1,)), ((), ()))
    bf16 = jnp.bfloat16
    own_w = 2 * SAMPLE_ROWS
    q = q_ref[...].astype(bf16)
    pad = jnp.zeros((own_w - SAMPLE_ROWS, MOBA_HD), jnp.float32)
    kn = jnp.concatenate([k_ref[...], pad], axis=0).astype(bf16)
    vn = jnp.concatenate([v_ref[...], pad], axis=0).astype(bf16)
    far_bias = tbl_ref[h * REL_BUCKETS + REL_BUCKETS - 1]
    last_blk = n_pages // ppb - 1
    r_i = lax.broadcasted_iota(jnp.int32, (SAMPLE_ROWS, own_w), 0)
    c_i = lax.broadcasted_iota(jnp.int32, (SAMPLE_ROWS, own_w), 1)
    own_bias = _bias_from_buckets(jnp.maximum(r_i - c_i, 0), tbl_ref, h)
    near = _bias_from_buckets(bk_ref[...], tbl_ref, h)
    n_key = MOBA_TOPK * MOBA_BLOCK + own_w
    row_sel = lax.broadcasted_iota(jnp.int32, (SAMPLE_ROWS, n_key), 0)
    key_col = lax.broadcasted_iota(jnp.int32, (1, n_key), 1)
    out = jnp.zeros((SAMPLE_ROWS, MOBA_HD), jnp.float32)
    out_row = lax.broadcasted_iota(jnp.int32, out.shape, 0)
    for t in range(t_real):
        keys = jnp.concatenate([ks_ref[slot, t].astype(bf16), kn], axis=0)
        vals = jnp.concatenate([vs_ref[slot, t].astype(bf16), vn], axis=0)
        bias = []
        for r in range(MOBA_TOPK):
            blk = idx_ref[(step * SAMPLE_ROWS + t) * MOBA_TOPK + r]
            bias.append(jnp.where(blk == last_blk, near[t:t + 1, :], far_bias))
        bias.append(own_bias[t:t + 1, :])
        s_all = lax.dot_general(q, keys, nt, preferred_element_type=jnp.float32)
        s_t = jnp.sum(jnp.where(row_sel == t, s_all, 0.0), axis=0, keepdims=True)
        own_col = key_col - MOBA_TOPK * MOBA_BLOCK
        visible = (own_col < 0) | ((own_col <= t) & (own_col < t_real))
        s_t = jnp.where(visible, s_t + jnp.concatenate(bias, axis=1), NEG_INF)
        m = jnp.max(s_t, axis=1, keepdims=True)
        p = jnp.exp2(s_t - m)
        den = jnp.sum(p, axis=1, keepdims=True)
        p8 = jnp.broadcast_to(p, (SAMPLE_ROWS, n_key)).astype(bf16)
        o_t = jnp.dot(p8, vals, preferred_element_type=jnp.float32) / den
        out = jnp.where(out_row == t, o_t, out)
    o_ref[...] = out.astype(o_ref.dtype)


def _sample_attn(page_table, sel_idx, rel_tbl, q_arr, q_blk0, k_all, v_all, cache_k, cache_v,
                 layer, t_real, out_dtype):
    ndb, n_pages = page_table.shape
    n = k_all.shape[1]
    past = n_pages * PAGE_SIZE
    last0 = past - MOBA_BLOCK
    t = np.arange(SAMPLE_ROWS)
    r = np.arange(MOBA_BLOCK)
    near_buckets = jnp.asarray(_rel_bucket_np(past + t[:, None] - (last0 + r[None, :])))
    grid_spec = pltpu.PrefetchScalarGridSpec(
        num_scalar_prefetch=3,
        grid=(ndb, MOBA_HEADS),
        in_specs=[
            pl.BlockSpec((SAMPLE_ROWS, MOBA_HD), lambda d, h, *_: (d, q_blk0 + h)),
            pl.BlockSpec((None, SAMPLE_ROWS, MOBA_HD), lambda d, h, *_: (layer, d, h)),
            pl.BlockSpec((None, SAMPLE_ROWS, MOBA_HD), lambda d, h, *_: (layer, d, h)),
            pl.BlockSpec((SAMPLE_ROWS, MOBA_BLOCK), lambda d, h, *_: (0, 0)),
            pl.BlockSpec(memory_space=pl.ANY),
            pl.BlockSpec(memory_space=pl.ANY),
        ],
        out_specs=pl.BlockSpec((SAMPLE_ROWS, MOBA_HD), lambda d, h, *_: (d, h)),
        scratch_shapes=[
            pltpu.VMEM((2, t_real, MOBA_TOPK * MOBA_BLOCK, MOBA_HD), jnp.float32),
            pltpu.VMEM((2, t_real, MOBA_TOPK * MOBA_BLOCK, MOBA_HD), jnp.float32),
            pltpu.SemaphoreType.DMA((2, 2)),
        ],
    )
    return pl.pallas_call(
        functools.partial(_sample_attn_kernel, n_pages, t_real, layer),
        out_shape=jax.ShapeDtypeStruct((n, D_MODEL), out_dtype),
        grid_spec=grid_spec,
        compiler_params=_cparams(("arbitrary", "arbitrary")),
        name="sample_attn",
    )(page_table.reshape(-1), sel_idx, rel_tbl, q_arr, k_all, v_all, near_buckets, cache_k,
      cache_v)


def _layer_norm(x, g, b):
    mu = jnp.mean(x, axis=-1, keepdims=True)
    xc = x - mu
    var = jnp.mean(xc * xc, axis=-1, keepdims=True)
    return xc * lax.rsqrt(var + LN_EPS) * g + b


def _mix_kernel(r_ref, m_ref, ga_ref, gb_ref, x_ref, wr_ref, wm_ref, wo_ref, g_ref, b_ref,
                wh_ref, wl_ref, x1_ref, ri_ref, rg_ref, cnt_ref, carry_ref):
    i = pl.program_id(0)

    @pl.when(i == 0)
    def _():
        carry_ref[...] = jnp.zeros_like(carry_ref)

    f32 = jnp.float32
    a = jnp.dot(r_ref[...].astype(jnp.bfloat16), wr_ref[...], preferred_element_type=f32)
    m = jnp.dot(m_ref[...].astype(jnp.bfloat16), wm_ref[...], preferred_element_type=f32)
    mix = _sigmoid(ga_ref[...].astype(f32)) * a + _sigmoid(gb_ref[...].astype(f32)) * m
    mixed = jnp.dot(mix.astype(jnp.bfloat16), wo_ref[...], preferred_element_type=f32)
    x1 = _layer_norm(DN_ALPHA * x_ref[...] + mixed, g_ref[...], b_ref[...])
    x1_ref[...] = x1

    xh = x1.astype(jnp.bfloat16)
    xl = (x1 - xh.astype(f32)).astype(jnp.bfloat16)
    logits = (jnp.dot(xh, wh_ref[...], preferred_element_type=f32)
              + jnp.dot(xl, wh_ref[...], preferred_element_type=f32)
              + jnp.dot(xh, wl_ref[...], preferred_element_type=f32))
    lane = lax.broadcasted_iota(jnp.int32, logits.shape, 1)
    is_g = lane < MOE_GROUPS
    lgm = jnp.where(is_g, logits, -jnp.inf)
    mg = jnp.max(lgm, axis=1, keepdims=True)
    grp = jnp.min(jnp.where(lgm == mg, lane, LANES), axis=1, keepdims=True)
    pg = 1.0 / jnp.sum(jnp.where(is_g, jnp.exp(logits - mg), 0.0), axis=1, keepdims=True)
    e_lane = lane - MOE_GROUPS
    in_grp = (e_lane >= 0) & (e_lane < MOE_EXPERTS) & ((e_lane >> 3) == grp)
    v1 = jnp.where(in_grp, logits, -jnp.inf)
    t1 = jnp.max(v1, axis=1, keepdims=True)
    i1 = jnp.min(jnp.where(v1 == t1, lane, LANES), axis=1, keepdims=True)
    v2 = jnp.where(lane == i1, -jnp.inf, v1)
    t2 = jnp.max(v2, axis=1, keepdims=True)
    i2 = jnp.min(jnp.where(v2 == t2, lane, LANES), axis=1, keepdims=True)
    z = jnp.exp(t2 - t1)
    g1 = pg / (1.0 + z)
    g2 = pg * z / (1.0 + z)
    oh = jnp.where((lane == i1) | (lane == i2), 1.0, 0.0)
    tm = oh.shape[0]
    tri = jnp.where(lax.broadcasted_iota(jnp.int32, (tm, tm), 0)
                    > lax.broadcasted_iota(jnp.int32, (tm, tm), 1), 1.0, 0.0)
    cum = jnp.dot(tri.astype(jnp.bfloat16), oh.astype(jnp.bfloat16),
                  preferred_element_type=f32) + carry_ref[...]
    r1 = jnp.sum(jnp.where(lane == i1, cum, 0.0), axis=1, keepdims=True)
    r2 = jnp.sum(jnp.where(lane == i2, cum, 0.0), axis=1, keepdims=True)
    carry_ref[...] = carry_ref[...] + jnp.sum(oh, axis=0, keepdims=True)
    ri = jnp.where(lane == 0, i1 - MOE_GROUPS,
                   jnp.where(lane == 1, i2 - MOE_GROUPS,
                             jnp.where(lane == 2, r1.astype(jnp.int32),
                                       jnp.where(lane == 3, r2.astype(jnp.int32), 0))))
    ri_ref[...] = ri
    rg_ref[...] = jnp.where(lane == 0, g1, jnp.where(lane == 1, g2, 0.0))
    cnt_ref[...] = carry_ref[...]


def _mix(ret_act, moba_o, gates, x, w_ret_o, w_moba_o, w_out, ln_g, ln_b, w_hi, w_lo, tm):
    n = x.shape[0]
    full = lambda shape: pl.BlockSpec(shape, lambda i: (0, 0))
    return pl.pallas_call(
        _mix_kernel,
        out_shape=(jax.ShapeDtypeStruct((n, D_MODEL), jnp.float32),
                   jax.ShapeDtypeStruct((n, LANES), jnp.int32),
                   jax.ShapeDtypeStruct((n, LANES), jnp.float32),
                   jax.ShapeDtypeStruct((1, LANES), jnp.float32)),
        grid=(n // tm,),
        in_specs=[
            pl.BlockSpec((tm, RET_W), lambda i: (i, 0)),
            pl.BlockSpec((tm, D_MODEL), lambda i: (i, 0)),
            pl.BlockSpec((tm, D_MODEL), lambda i: (i, 0)),
            pl.BlockSpec((tm, D_MODEL), lambda i: (i, 1)),
            pl.BlockSpec((tm, D_MODEL), lambda i: (i, 0)),
            full((RET_W, D_MODEL)), full((D_MODEL, D_MODEL)), full((D_MODEL, D_MODEL)),
            full((1, D_MODEL)), full((1, D_MODEL)),
            full((D_MODEL, LANES)), full((D_MODEL, LANES)),
        ],
        out_specs=(pl.BlockSpec((tm, D_MODEL), lambda i: (i, 0)),
                   pl.BlockSpec((tm, LANES), lambda i: (i, 0)),
                   pl.BlockSpec((tm, LANES), lambda i: (i, 0)),
                   pl.BlockSpec((1, LANES), lambda i: (0, 0))),
        scratch_shapes=[pltpu.VMEM((1, LANES), jnp.float32)],
        compiler_params=_cparams(("arbitrary",)),
        name="mix_ln_router",
    )(ret_act, moba_o, gates, gates, x, w_ret_o, w_moba_o, w_out, ln_g, ln_b, w_hi, w_lo)


def _stage_dest(dest_hbm, dest_ref, sem):
    i = pl.program_id(0)
    slot = lax.rem(i, 2)

    def copy(step, buf):
        return pltpu.make_async_copy(dest_hbm.at[step], dest_ref.at[buf], sem.at[buf])

    @pl.when(i == 0)
    def _():
        copy(i, slot).start()

    @pl.when(i + 1 < pl.num_programs(0))
    def _():
        copy(i + 1, 1 - slot).start()

    copy(i, slot).wait()
    return slot


def _dest_at(dest_ref, slot, t, k):
    e = 2 * t + k
    return dest_ref[slot, lax.shift_right_logical(e, 7), lax.bitwise_and(e, LANES - 1)]


def _dispatch_kernel(tm, dest_hbm, x_ref, z_hbm, o_hbm, dest_ref, sem, isem):
    del z_hbm
    slot = _stage_dest(dest_hbm, dest_ref, isem)

    def row_copy(t, k):
        return pltpu.make_async_copy(x_ref.at[pl.ds(t, 1)],
                                     o_hbm.at[pl.ds(_dest_at(dest_ref, slot, t, k), 1)], sem)

    def start(t, carry):
        row_copy(t, 0).start()
        row_copy(t, 1).start()
        return carry

    def wait(t, carry):
        row_copy(t, 0).wait()
        row_copy(t, 1).wait()
        return carry

    lax.fori_loop(0, tm, start, 0, unroll=ROW_DMA_UNROLL)
    lax.fori_loop(0, tm, wait, 0, unroll=ROW_DMA_UNROLL)


def _dispatch(x1, dest, n_rows, tm):
    n = x1.shape[0]
    return pl.pallas_call(
        functools.partial(_dispatch_kernel, tm),
        out_shape=jax.ShapeDtypeStruct((n_rows, D_MODEL), jnp.float32),
        grid=(n // tm,),
        in_specs=[pl.BlockSpec(memory_space=pl.ANY),
                  pl.BlockSpec((tm, D_MODEL), lambda i: (i, 0)),
                  pl.BlockSpec(memory_space=pl.ANY)],
        out_specs=pl.BlockSpec(memory_space=pl.ANY),
        scratch_shapes=[pltpu.SMEM((2, 2 * tm // LANES, LANES), jnp.int32),
                        pltpu.SemaphoreType.DMA, pltpu.SemaphoreType.DMA((2,))],
        input_output_aliases={2: 0},
        compiler_params=_cparams(("arbitrary",)),
        name="moe_dispatch",
    )(dest, x1, jnp.zeros((n_rows, D_MODEL), jnp.float32))


def _experts_kernel(be_ref, nu_ref, x_ref, wg_ref, wu_ref, wd_ref, o_ref, wgb, wub, wdb):
    i = pl.program_id(0)
    used = i < nu_ref[0]
    first = (i == 0) | (be_ref[i] != be_ref[jnp.maximum(i - 1, 0)])

    @pl.when(used & first)
    def _():
        wgb[...] = wg_ref[...].astype(jnp.bfloat16)
        wub[...] = wu_ref[...].astype(jnp.bfloat16)
        wdb[...] = wd_ref[...].astype(jnp.bfloat16)

    @pl.when(used)
    def _():
        xb = x_ref[...].astype(jnp.bfloat16)
        g = jnp.dot(xb, wgb[...], preferred_element_type=jnp.float32)
        u = jnp.dot(xb, wub[...], preferred_element_type=jnp.float32)
        hid = (g * _sigmoid(g) * u).astype(jnp.bfloat16)
        o_ref[...] = jnp.dot(hid, wdb[...], preferred_element_type=jnp.float32)

    @pl.when(jnp.logical_not(used))
    def _():
        o_ref[...] = jnp.zeros_like(o_ref)


def _experts(xs, blk_exp, n_used, w_gate, w_up, w_down, layer):
    n_rows = xs.shape[0]
    r = MOE_ROWS
    grid_spec = pltpu.PrefetchScalarGridSpec(
        num_scalar_prefetch=2,
        grid=(n_rows // r,),
        in_specs=[
            pl.BlockSpec((r, D_MODEL), lambda i, be, nu: (i, 0)),
            pl.BlockSpec((None, None, D_MODEL, MOE_FF), lambda i, be, nu: (layer, be[i], 0, 0)),
            pl.BlockSpec((None, None, D_MODEL, MOE_FF), lambda i, be, nu: (layer, be[i], 0, 0)),
            pl.BlockSpec((None, None, MOE_FF, D_MODEL), lambda i, be, nu: (layer, be[i], 0, 0)),
        ],
        out_specs=pl.BlockSpec((r, D_MODEL), lambda i, be, nu: (i, 0)),
        scratch_shapes=[pltpu.VMEM((D_MODEL, MOE_FF), jnp.bfloat16),
                        pltpu.VMEM((D_MODEL, MOE_FF), jnp.bfloat16),
                        pltpu.VMEM((MOE_FF, D_MODEL), jnp.bfloat16)],
    )
    return pl.pallas_call(
        _experts_kernel,
        out_shape=jax.ShapeDtypeStruct((n_rows, D_MODEL), jnp.float32),
        grid_spec=grid_spec,
        compiler_params=_cparams(("arbitrary",)),
        name="moe_experts",
    )(blk_exp, n_used, xs, w_gate, w_up, w_down)


def _combine_kernel(tm, dest_hbm, x_ref, gt_ref, g_ref, b_ref, y_hbm, o_ref, ob_ref,
                    ya_ref, yb_ref, dest_ref, sem, isem):
    slot = _stage_dest(dest_hbm, dest_ref, isem)

    def row_copy(t, k):
        buf = ya_ref if k == 0 else yb_ref
        return pltpu.make_async_copy(y_hbm.at[pl.ds(_dest_at(dest_ref, slot, t, k), 1)],
                                     buf.at[pl.ds(t, 1)], sem.at[k])

    def start(t, carry):
        row_copy(t, 0).start()
        row_copy(t, 1).start()
        return carry

    def wait(t, carry):
        row_copy(t, 0).wait()
        row_copy(t, 1).wait()
        return carry

    lax.fori_loop(0, tm, start, 0, unroll=ROW_DMA_UNROLL)
    lax.fori_loop(0, tm, wait, 0, unroll=ROW_DMA_UNROLL)
    gt = gt_ref[...]
    y = gt[:, 0:1] * ya_ref[...] + gt[:, 1:2] * yb_ref[...]
    x2 = _layer_norm(DN_ALPHA * x_ref[...] + y, g_ref[...], b_ref[...])
    o_ref[...] = x2
    ob_ref[...] = x2.astype(jnp.bfloat16)


def _combine(x1, gates, dest, ys, ln_g, ln_b, tm):
    n = x1.shape[0]
    return pl.pallas_call(
        functools.partial(_combine_kernel, tm),
        out_shape=(jax.ShapeDtypeStruct((n, D_MODEL), jnp.float32),
                   jax.ShapeDtypeStruct((n, D_MODEL), jnp.bfloat16)),
        grid=(n // tm,),
        in_specs=[pl.BlockSpec(memory_space=pl.ANY),
                  pl.BlockSpec((tm, D_MODEL), lambda i: (i, 0)),
                  pl.BlockSpec((tm, LANES), lambda i: (i, 0)),
                  pl.BlockSpec((1, D_MODEL), lambda i: (0, 0)),
                  pl.BlockSpec((1, D_MODEL), lambda i: (0, 0)),
                  pl.BlockSpec(memory_space=pl.ANY)],
        out_specs=(pl.BlockSpec((tm, D_MODEL), lambda i: (i, 0)),
                   pl.BlockSpec((tm, D_MODEL), lambda i: (i, 0))),
        scratch_shapes=[pltpu.VMEM((tm, D_MODEL), jnp.float32),
                        pltpu.VMEM((tm, D_MODEL), jnp.float32),
                        pltpu.SMEM((2, 2 * tm // LANES, LANES), jnp.int32),
                        pltpu.SemaphoreType.DMA((2,)), pltpu.SemaphoreType.DMA((2,))],
        compiler_params=_cparams(("arbitrary",)),
        name="moe_combine_ln",
    )(dest, x1, gates, ln_g, ln_b, ys)


def _moe(x1, route_i, route_g, counts, w_gate, w_up, w_down, layer, ln_g, ln_b, tm):
    n = x1.shape[0]
    r = MOE_ROWS
    n_blk = (2 * n) // r + MOE_EXPERTS
    cnt = counts[0, MOE_GROUPS:MOE_GROUPS + MOE_EXPERTS].astype(jnp.int32)
    padded = (cnt + r - 1) // r * r
    pad_end = jnp.cumsum(padded)
    pad_start = pad_end - padded
    dest = (pad_start[route_i[:, 0:2]] + route_i[:, 2:4]).astype(jnp.int32)
    dest = dest.reshape(n // tm, 2 * tm // LANES, LANES)
    blk_start = jnp.arange(n_blk, dtype=jnp.int32) * r
    blk_exp = jnp.minimum(jnp.sum(pad_end[None, :] <= blk_start[:, None], axis=1),
                          MOE_EXPERTS - 1).astype(jnp.int32)
    n_used = (pad_end[-1:] // r).astype(jnp.int32)
    xs = _dispatch(x1, dest, n_blk * r, tm)
    ys = _experts(xs, blk_exp, n_used, w_gate, w_up, w_down, layer)
    return _combine(x1, route_g, dest, ys, ln_g, ln_b, tm)


def _split_hi_lo(w):
    hi = w.astype(jnp.bfloat16)
    return hi, (w - hi.astype(jnp.float32)).astype(jnp.bfloat16)


def _layer_common(x, xb, k_prev, v_prev, attn_fn, lw, layer, tm_proj, tm, tm_moe, act_dtype):
    (w_in, col_scale, w_ret_o, w_moba_o, w_out, ln1_g, ln1_b, w_hi, w_lo,
     w_gate, w_up, w_down, ln2_g, ln2_b) = lw
    proj = functools.partial(_proj, xb, w_in, layer, col_scale)
    qk = proj(COL_RQ, 2, jnp.float32, tm_proj)
    vgq = proj(COL_RV, 5, act_dtype, tm_proj)
    k_all = _proj_into(xb, w_in, layer, col_scale, COL_MK, k_prev, tm_proj)
    v_all = _proj_into(xb, w_in, layer, col_scale, COL_MK + 1, v_prev, tm_proj)
    gates = proj(COL_GA, 2, act_dtype, tm_proj)
    ret_act, ret_state, moba_o = attn_fn(qk, vgq, k_all, v_all)
    x1, route_i, route_g, counts = _mix(ret_act, moba_o, gates, x, w_ret_o, w_moba_o, w_out,
                                        ln1_g, ln1_b, w_hi, w_lo, tm)
    x2, x2b = _moe(x1, route_i, route_g, counts, w_gate, w_up, w_down, layer, ln2_g, ln2_b,
                   tm_moe)
    return x2, x2b, k_all, v_all, ret_state


def kernel(x_prompt, x_sample, cache_k, cache_v, state_ret, page_table, rel_bias, w_in, ret_gn_w,
           w_ret_o, w_moba_o, w_out, ln1_g, ln1_b, w_group, w_router, w_exp_gate, w_exp_up,
           w_exp_down, ln2_g, ln2_b):
    b, s, _ = x_prompt.shape
    db, t_real, _ = x_sample.shape
    n_pages = page_table.shape[1]
    past = n_pages * PAGE_SIZE
    bf16 = jnp.bfloat16

    col_scale = np.ones((1, PROJ_TILES * D_MODEL), np.float32)
    col_scale[0, D_MODEL:2 * D_MODEL] = RET_DK ** -0.5
    col_scale[0, COL_MQ * D_MODEL:(COL_MQ + 1) * D_MODEL] = MOBA_HD ** -0.5 * LOG2E
    col_scale = jnp.asarray(col_scale)
    rel_tbl = (rel_bias.T * LOG2E).reshape(-1)

    cos_p, sin_p = _xpos_tables(jnp.arange(s))
    pos_s = jnp.minimum(past + jnp.arange(SAMPLE_ROWS), past + t_real - 1)
    cos_s, sin_s = _xpos_tables(pos_s)

    xp = x_prompt.reshape(b * s, D_MODEL)
    xs = jnp.pad(x_sample, ((0, 0), (0, SAMPLE_ROWS - t_real), (0, 0))).reshape(
        db * SAMPLE_ROWS, D_MODEL)
    xpb, xsb = xp.astype(bf16), xs.astype(bf16)
    zero_state = jnp.zeros((b, RET_HEADS, RET_DK, RET_DV), jnp.float32)
    state_all = state_ret.reshape(DEPTH * db, RET_HEADS, RET_DK, RET_DV)
    mq_blk = (COL_MQ - COL_RV) * D_MODEL // MOBA_HD

    rp = []
    kp, vp = (jnp.zeros((DEPTH, b * s, D_MODEL), jnp.float32) for _ in range(2))
    ks, vs = (jnp.zeros((DEPTH, db * SAMPLE_ROWS, D_MODEL), jnp.float32) for _ in range(2))
    st_s = jnp.zeros((DEPTH * db, RET_HEADS, RET_DK, RET_DV), jnp.float32)
    for l in range(DEPTH):
        w_gr = jnp.zeros((D_MODEL, LANES), jnp.float32)
        w_gr = w_gr.at[:, :MOE_GROUPS].set(w_group[l])
        w_gr = w_gr.at[:, MOE_GROUPS:MOE_GROUPS + MOE_EXPERTS].set(w_router[l])
        w_hi, w_lo = _split_hi_lo(w_gr)
        lw = (w_in, col_scale, w_ret_o[l].astype(bf16), w_moba_o[l].astype(bf16),
              w_out[l].astype(bf16), ln1_g[l][None], ln1_b[l][None], w_hi, w_lo,
              w_exp_gate, w_exp_up, w_exp_down, ln2_g[l][None], ln2_b[l][None])
        gn = ret_gn_w[l][None]

        page_sums = []

        def prompt_attn(qk, vgq, k_all, v_all, l=l, page_sums=page_sums):
            rc = min(RET_STEP_ROWS, s)
            ret_act, st = _retention(qk, vgq, 0, 1, cos_p, sin_p, gn, zero_state, 0,
                                     b, s // rc, rc, rc, bf16)
            moba_o, psums = _moba_prompt(vgq, mq_blk, k_all, v_all, l, rel_tbl, cache_k, b, s,
                                         bf16)
            page_sums.append(psums)
            return ret_act, st, moba_o

        def sample_attn(qk, vgq, k_all, v_all, l=l, st_prev=st_s, page_sums=page_sums):
            ret_act, st = _retention(qk, vgq, 0, 1, cos_s, sin_s, gn, state_all, l * db,
                                     db, 1, SAMPLE_ROWS, t_real, jnp.float32,
                                     sf_rows=DEPTH * db, sf_off=l * db, sf_prev=st_prev)
            sel = _sample_select(page_table, vgq, COL_MQ - COL_RV, page_sums[0])
            sel_idx = sel[:, :, :, :MOBA_TOPK].reshape(-1)
            moba_o = _sample_attn(page_table, sel_idx, rel_tbl, vgq, mq_blk, k_all, v_all,
                                  cache_k, cache_v, l, t_real, jnp.float32)
            return ret_act, st, moba_o

        n_s = db * SAMPLE_ROWS
        xp, xpb, kp, vp, st_p = _layer_common(xp, xpb, kp, vp, prompt_attn, lw, l,
                                              min(1024, b * s), 256, 512, bf16)
        tm_s = min(256, n_s)
        xs, xsb, ks, vs, st_s = _layer_common(xs, xsb, ks, vs, sample_attn, lw, l, tm_s, tm_s,
                                              tm_s, jnp.float32)
        rp.append(st_p)

    page_shape = (DEPTH, b, s // PAGE_SIZE, PAGE_SIZE, MOBA_HEADS, MOBA_HD)
    new_shape = (DEPTH, db, SAMPLE_ROWS, MOBA_HEADS, MOBA_HD)
    yp = xp.reshape(b, s, D_MODEL)
    ys = xs.reshape(db, SAMPLE_ROWS, D_MODEL)[:, :t_real]
    return (yp, ys, kp.reshape(page_shape), vp.reshape(page_shape), jnp.stack(rp),
            ks.reshape(new_shape)[:, :, :t_real], vs.reshape(new_shape)[:, :, :t_real],
            st_s.reshape(DEPTH, db, RET_HEADS, RET_DK, RET_DV))
```

```python
import functools
import math

import numpy as np
import jax
import jax.numpy as jnp
from jax import lax
from jax.experimental import pallas as pl
from jax.experimental.pallas import tpu as pltpu

D_MODEL = 1024
DEPTH = 2
PAGE_SIZE = 128
RET_HEADS = 4
RET_DK = D_MODEL // RET_HEADS
RET_DV = 2 * D_MODEL // RET_HEADS
RET_W = RET_HEADS * RET_DV
XPOS_BASE = 10000.0
GN_EPS = 1e-6
MOBA_HEADS = 8
MOBA_HD = D_MODEL // MOBA_HEADS
MOBA_BLOCK = 256
MOBA_TOPK = 3
REL_BUCKETS = 32
REL_MAX_DIST = 128
MOE_GROUPS = 4
MOE_EPG = 8
MOE_EXPERTS = MOE_GROUPS * MOE_EPG
MOE_FF = D_MODEL // 2
DN_ALPHA = (2.0 * DEPTH) ** 0.25
LN_EPS = 1e-5
NEG_INF = -1e30
LOG2E = math.log2(math.e)

COL_RQ, COL_RV, COL_RG, COL_MQ, COL_MK, COL_GA = 0, 2, 4, 6, 7, 9
PROJ_TILES = 11

LANES = 128
SUBLANES = 8
VMEM_LIMIT = 56 * 1024 * 1024
SAMPLE_ROWS = 8
RET_STEP_ROWS = 512
MOE_ROWS = 256
ROW_DMA_UNROLL = 8
MOBA_ONES_ROWS = 16
MOBA_GROUP = 4


def _cparams(sem):
    return pltpu.CompilerParams(dimension_semantics=sem, vmem_limit_bytes=VMEM_LIMIT)


def _sigmoid(x):
    return 1.0 / (1.0 + jnp.exp(-x))


def _proj_kernel(x_ref, w_ref, s_ref, o_ref, wb_ref):
    @pl.when(pl.program_id(1) == 0)
    def _():
        wb_ref[...] = w_ref[...].astype(jnp.bfloat16)

    acc = jnp.dot(x_ref[...], wb_ref[...], preferred_element_type=jnp.float32)
    o_ref[...] = (acc * s_ref[...]).astype(o_ref.dtype)


def _proj(xb, w_in, layer, col_scale, col0, ncols, out_dtype, tm):
    n = xb.shape[0]
    tn = D_MODEL
    return pl.pallas_call(
        _proj_kernel,
        out_shape=jax.ShapeDtypeStruct((n, ncols * tn), out_dtype),
        grid=(ncols, n // tm),
        in_specs=[
            pl.BlockSpec((tm, D_MODEL), lambda j, i: (i, 0)),
            pl.BlockSpec((None, D_MODEL, tn), lambda j, i: (layer, 0, j + col0)),
            pl.BlockSpec((1, tn), lambda j, i: (0, j + col0)),
        ],
        out_specs=pl.BlockSpec((tm, tn), lambda j, i: (i, j)),
        scratch_shapes=[pltpu.VMEM((D_MODEL, tn), jnp.bfloat16)],
        compiler_params=_cparams(("arbitrary", "arbitrary")),
        name="in_proj",
    )(xb, w_in, col_scale)


def _proj_layer_kernel(x_ref, w_ref, s_ref, prev_ref, o_ref, wb_ref):
    del prev_ref
    _proj_kernel(x_ref, w_ref, s_ref, o_ref, wb_ref)


def _proj_into(xb, w_in, layer, col_scale, col, prev, tm):
    n = xb.shape[0]
    tn = D_MODEL
    return pl.pallas_call(
        _proj_layer_kernel,
        out_shape=jax.ShapeDtypeStruct((DEPTH, n, tn), jnp.float32),
        grid=(1, n // tm),
        in_specs=[
            pl.BlockSpec((tm, D_MODEL), lambda j, i: (i, 0)),
            pl.BlockSpec((None, D_MODEL, tn), lambda j, i: (layer, 0, col)),
            pl.BlockSpec((1, tn), lambda j, i: (0, col)),
            pl.BlockSpec(memory_space=pl.ANY),
        ],
        out_specs=pl.BlockSpec((None, tm, tn), lambda j, i: (layer, i, 0)),
        scratch_shapes=[pltpu.VMEM((D_MODEL, tn), jnp.bfloat16)],
        input_output_aliases={3: 0},
        compiler_params=_cparams(("arbitrary", "arbitrary")),
        name="in_proj_kv",
    )(xb, w_in, col_scale, prev)


def _ret_consts(c, c_true):
    lg = np.log1p(-np.exp2(-5.0 - np.arange(RET_HEADS, dtype=np.float64)))
    j = np.arange(c, dtype=np.float64)
    diff = j[:, None] - j[None, :]
    dmask = np.where(diff >= 0, np.exp(lg[:, None, None] * np.maximum(diff, 0.0)), 0.0)
    qdec = np.exp(lg[:, None] * (j + 1.0))[..., None]
    kdec = np.where(j < c_true, np.exp(lg[:, None] * (c_true - 1.0 - j)), 0.0)[..., None]
    sdec = [float(v) for v in np.exp(lg * c_true)]
    return (jnp.asarray(dmask, jnp.float32), jnp.asarray(qdec, jnp.float32),
            jnp.asarray(kdec, jnp.float32), sdec)


def _ret_kernel(sdec, q_ref, k_ref, v_ref, g_ref, cos_ref, sin_ref, dm_ref, qd_ref, kd_ref,
                gn_ref, s0_ref, o_ref, sf_ref, st_ref):
    c = pl.program_id(1)

    @pl.when(c == 0)
    def _():
        st_ref[...] = s0_ref[0]

    cos = cos_ref[...]
    sin = sin_ref[...]
    half = RET_DK // 2

    def rot(x):
        x1, x2 = x[:, :half], x[:, half:]
        return jnp.concatenate([x1 * cos - x2 * sin, x1 * sin + x2 * cos], axis=-1)

    for h in range(RET_HEADS):
        qr = rot(q_ref[:, h * RET_DK:(h + 1) * RET_DK])
        kr = rot(k_ref[:, h * RET_DK:(h + 1) * RET_DK])
        qb = qr.astype(jnp.bfloat16)
        kb = kr.astype(jnp.bfloat16)
        v = v_ref[:, h * RET_DV:(h + 1) * RET_DV].astype(jnp.bfloat16)
        st = st_ref[h]
        s = lax.dot_general(qb, kb, (((1,), (1,)), ((), ())),
                            preferred_element_type=jnp.float32) * dm_ref[h]
        inner = jnp.dot(s.astype(jnp.bfloat16), v, preferred_element_type=jnp.float32)
        cross = jnp.dot(qb, st.astype(jnp.bfloat16),
                        preferred_element_type=jnp.float32) * qd_ref[h]
        kdb = (kr * kd_ref[h]).astype(jnp.bfloat16)
        st_ref[h] = sdec[h] * st + lax.dot_general(
            kdb, v, (((0,), (0,)), ((), ())), preferred_element_type=jnp.float32)
        o = inner + cross
        mu = jnp.mean(o, axis=-1, keepdims=True)
        oc = o - mu
        var = jnp.mean(oc * oc, axis=-1, keepdims=True)
        on = oc * lax.rsqrt(var + GN_EPS) * gn_ref[:, h * RET_DV:(h + 1) * RET_DV]
        g = g_ref[:, h * RET_DV:(h + 1) * RET_DV].astype(jnp.float32)
        o_ref[:, h * RET_DV:(h + 1) * RET_DV] = (on * (g * _sigmoid(g))).astype(o_ref.dtype)

    @pl.when(c == pl.num_programs(1) - 1)
    def _():
        sf_ref[0] = st_ref[...]


def _ret_layer_kernel(sdec, *refs):
    _ret_kernel(sdec, *refs[:11], *refs[12:])


def _retention(qk, vg, v_blk, g_blk, cos, sin, gn_w, state0, s0_off, nb, nc, c, c_true,
               out_dtype, sf_rows=None, sf_off=0, sf_prev=None):
    n = qk.shape[0]
    sf_rows = nb if sf_rows is None else sf_rows
    dmask, qdec, kdec, sdec = _ret_consts(c, c_true)
    row = lambda b, i: b * nc + i
    extra_specs = [] if sf_prev is None else [pl.BlockSpec(memory_space=pl.ANY)]
    extra_args = [] if sf_prev is None else [sf_prev]
    return pl.pallas_call(
        functools.partial(_ret_kernel if sf_prev is None else _ret_layer_kernel, sdec),
        out_shape=(jax.ShapeDtypeStruct((n, RET_W), out_dtype),
                   jax.ShapeDtypeStruct((sf_rows, RET_HEADS, RET_DK, RET_DV), jnp.float32)),
        grid=(nb, nc),
        input_output_aliases={} if sf_prev is None else {11: 1},
        in_specs=[
            pl.BlockSpec((c, D_MODEL), lambda b, i: (row(b, i), 0)),
            pl.BlockSpec((c, D_MODEL), lambda b, i: (row(b, i), 1)),
            pl.BlockSpec((c, RET_W), lambda b, i: (row(b, i), v_blk)),
            pl.BlockSpec((c, RET_W), lambda b, i: (row(b, i), g_blk)),
            pl.BlockSpec((c, RET_DK // 2), lambda b, i: (i, 0)),
            pl.BlockSpec((c, RET_DK // 2), lambda b, i: (i, 0)),
            pl.BlockSpec((RET_HEADS, c, c), lambda b, i: (0, 0, 0)),
            pl.BlockSpec((RET_HEADS, c, 1), lambda b, i: (0, 0, 0)),
            pl.BlockSpec((RET_HEADS, c, 1), lambda b, i: (0, 0, 0)),
            pl.BlockSpec((1, RET_W), lambda b, i: (0, 0)),
            pl.BlockSpec((1, RET_HEADS, RET_DK, RET_DV), lambda b, i: (s0_off + b, 0, 0, 0)),
        ] + extra_specs,
        out_specs=(pl.BlockSpec((c, RET_W), lambda b, i: (row(b, i), 0)),
                   pl.BlockSpec((1, RET_HEADS, RET_DK, RET_DV),
                                lambda b, i: (sf_off + b, 0, 0, 0))),
        scratch_shapes=[pltpu.VMEM((RET_HEADS, RET_DK, RET_DV), jnp.float32)],
        compiler_params=_cparams(("arbitrary", "arbitrary")),
        name="retention",
    )(qk, qk, vg, vg, cos, sin, dmask, qdec, kdec, gn_w, state0, *extra_args)


def _xpos_tables(pos):
    half = RET_DK // 2
    inv = 1.0 / (XPOS_BASE ** jnp.linspace(0.0, 1.0, half, dtype=jnp.float32))
    ang = pos.astype(jnp.float32)[:, None] * inv[None, :]
    return jnp.cos(ang), jnp.sin(ang)


def _rel_bucket_np(rel):
    n = np.maximum(rel, 0)
    exact = REL_BUCKETS // 2
    nf = np.maximum(n, 1).astype(np.float32)
    large = exact + (np.log(nf / np.float32(exact)) / np.float32(math.log(REL_MAX_DIST / exact))
                     * np.float32(REL_BUCKETS - exact)).astype(np.int32)
    return np.where(n < exact, n, np.minimum(large, REL_BUCKETS - 1)).astype(np.int32)


def _bias_from_buckets(bucket, table_ref, h):
    out = jnp.zeros(bucket.shape, jnp.float32)
    for b in range(REL_BUCKETS):
        out = jnp.where(bucket == b, table_ref[h * REL_BUCKETS + b], out)
    return out


def _top3_mask(scores, valid, row, nrow):
    sc = jnp.where(valid, scores, NEG_INF)
    sel = jnp.zeros(scores.shape, jnp.float32)
    for _ in range(MOBA_TOPK):
        m = jnp.max(sc, axis=0, keepdims=True)
        idx = jnp.min(jnp.where(sc == m, row, nrow), axis=0, keepdims=True)
        pick = row == idx
        sel = jnp.where(pick, 1.0, sel)
        sc = jnp.where(pick, -jnp.inf, sc)
    return jnp.where(valid, sel, 0.0)


def _moba_prompt_kernel(nblk, grp, tbl_ref, q_ref, k_ref, v_ref, bk_ref, c_ref, o_ref, ps_ref,
                        kb_ref, vt_ref, mh_ref, ml_ref, bown_ref, bprev_ref, sel_ref, acc_ref,
                        sa_ref, sb_ref):
    h = pl.program_id(1)
    qi = pl.program_id(2)
    blk = MOBA_BLOCK
    bf16 = jnp.bfloat16
    f32 = jnp.float32
    nt = (((1,), (1,)), ((), ()))

    @pl.when(qi == 0)
    def _():
        kf = k_ref[...]
        kb_ref[...] = kf.astype(bf16)
        means = jnp.mean(kf.reshape(nblk, blk, MOBA_HD), axis=1)
        mh = means.astype(bf16)
        mh_ref[...] = mh
        ml_ref[...] = (means - mh.astype(f32)).astype(bf16)

        def transpose_block(j, carry):
            r0 = pl.multiple_of(j * blk, blk)
            vt_ref[j, :MOBA_HD, :] = v_ref[pl.ds(r0, blk), :].T.astype(bf16)
            vt_ref[j, MOBA_HD:, :] = jnp.ones((MOBA_ONES_ROWS, blk), bf16)
            return carry

        lax.fori_loop(0, nblk, transpose_block, 0)
        far = tbl_ref[h * REL_BUCKETS + REL_BUCKETS - 1]
        bown_ref[...] = _bias_from_buckets(bk_ref[0], tbl_ref, h) - far
        bprev_ref[...] = _bias_from_buckets(bk_ref[1], tbl_ref, h) - far

    ps_ref[...] = jnp.sum(c_ref[...], axis=1)

    q = q_ref[...]
    tq = q.shape[0]
    scores = (lax.dot_general(mh_ref[...], q, nt, preferred_element_type=f32)
              + lax.dot_general(ml_ref[...], q, nt, preferred_element_type=f32))
    row = lax.broadcasted_iota(jnp.int32, scores.shape, 0)
    sel_ref[...] = _top3_mask(scores, row < qi, row, nblk)

    def softmax_step(tiles, m_old):
        m_new = m_old
        for s, keep, _ in tiles:
            m_new = jnp.maximum(m_new, jnp.where(keep, jnp.max(s, axis=0, keepdims=True), NEG_INF))
        upd = None
        for s, keep, vt in tiles:
            p = jnp.exp2(s - jnp.where(keep, m_new, -NEG_INF))
            d = jnp.dot(vt, p.astype(bf16), preferred_element_type=f32)
            upd = d if upd is None else upd + d
        acc_ref[...] = jnp.exp2(m_old - m_new) * acc_ref[...] + upd
        return m_new

    def picked(n, ok):
        return jnp.where(ok, sel_ref[pl.ds(n, 1), :], 0.0) > 0.0

    def group_base(gi):
        return jnp.minimum(gi * grp, nblk - grp)

    def group_logits(gi):
        r0 = pl.multiple_of(group_base(gi) * blk, blk)
        return lax.dot_general(kb_ref[pl.ds(r0, grp * blk), :], q, nt, preferred_element_type=f32)

    sa_ref[...] = group_logits(0)

    acc_ref[...] = jnp.zeros_like(acc_ref)
    prev = jnp.maximum(qi - 1, 0)
    s_prev = lax.dot_general(kb_ref[pl.ds(pl.multiple_of(prev * blk, blk), blk), :], q, nt,
                             preferred_element_type=f32) + bprev_ref[...]
    s_own = lax.dot_general(kb_ref[pl.ds(pl.multiple_of(qi * blk, blk), blk), :], q, nt,
                            preferred_element_type=f32) + bown_ref[...]
    k_i = lax.broadcasted_iota(jnp.int32, s_own.shape, 0)
    q_i = lax.broadcasted_iota(jnp.int32, s_own.shape, 1)
    s_own = jnp.where(q_i >= k_i, s_own, NEG_INF)
    everyone = jnp.full((1, tq), True)
    m = softmax_step([(s_prev, picked(prev, qi >= 1), vt_ref[prev]),
                      (s_own, everyone, vt_ref[qi])], jnp.full((1, tq), NEG_INF, f32))

    n_far = jnp.maximum(qi - 1, 0)

    def far_group(gi, s_ref, m_old):
        first = gi * grp
        base = group_base(gi)
        tiles = []
        for i in range(grp):
            n = base + i
            tiles.append((s_ref[i * blk:(i + 1) * blk, :],
                          picked(n, (n >= first) & (n < n_far)), vt_ref[n]))
        return softmax_step(tiles, m_old)

    def far_pair(j, m_old):
        sb_ref[...] = group_logits(2 * j + 1)
        m_mid = far_group(2 * j, sa_ref, m_old)
        sa_ref[...] = group_logits(2 * j + 2)
        return far_group(2 * j + 1, sb_ref, m_mid)

    n_grp = (n_far + grp - 1) // grp
    lax.fori_loop(0, (n_grp + 1) // 2, far_pair, m)
    acc = acc_ref[...]
    o_ref[...] = (acc[:MOBA_HD] / acc[MOBA_HD:MOBA_HD + 1]).T.astype(o_ref.dtype)


def _moba_prompt(q_arr, q_blk0, k_all, v_all, layer, rel_tbl, cache_k, nb, s, out_dtype):
    n = k_all.shape[1]
    blk = MOBA_BLOCK
    nblk = s // blk
    grp = min(MOBA_GROUP, nblk)
    n_phys = cache_k.shape[1]
    ppstep = n_phys // (nb * MOBA_HEADS * nblk)
    assert ppstep * nb * MOBA_HEADS * nblk == n_phys, "cache pages must split evenly over steps"
    step = lambda b, h, qi: (b * MOBA_HEADS + h) * nblk + qi
    i = np.arange(blk)
    buckets = jnp.asarray(np.stack([_rel_bucket_np(i[None, :] - i[:, None]),
                                    _rel_bucket_np(blk + i[None, :] - i[:, None])]))
    grid_spec = pltpu.PrefetchScalarGridSpec(
        num_scalar_prefetch=1,
        grid=(nb, MOBA_HEADS, nblk),
        in_specs=[
            pl.BlockSpec((blk, MOBA_HD), lambda b, h, qi, t: (b * nblk + qi, q_blk0 + h)),
            pl.BlockSpec((None, s, MOBA_HD), lambda b, h, qi, t: (layer, b, h)),
            pl.BlockSpec((None, s, MOBA_HD), lambda b, h, qi, t: (layer, b, h)),
            pl.BlockSpec((2, blk, blk), lambda b, h, qi, t: (0, 0, 0)),
            pl.BlockSpec((None, ppstep, PAGE_SIZE, MOBA_HEADS, MOBA_HD),
                         lambda b, h, qi, t: (layer, step(b, h, qi), 0, 0, 0)),
        ],
        out_specs=(pl.BlockSpec((blk, MOBA_HD), lambda b, h, qi, t: (b * nblk + qi, h)),
                   pl.BlockSpec((ppstep, MOBA_HEADS, MOBA_HD),
                                lambda b, h, qi, t: (step(b, h, qi), 0, 0))),
        scratch_shapes=[
            pltpu.VMEM((s, MOBA_HD), jnp.bfloat16),
            pltpu.VMEM((nblk, MOBA_HD + MOBA_ONES_ROWS, blk), jnp.bfloat16),
            pltpu.VMEM((nblk, MOBA_HD), jnp.bfloat16),
            pltpu.VMEM((nblk, MOBA_HD), jnp.bfloat16),
            pltpu.VMEM((blk, blk), jnp.float32),
            pltpu.VMEM((blk, blk), jnp.float32),
            pltpu.VMEM((nblk, blk), jnp.float32),
            pltpu.VMEM((MOBA_HD + MOBA_ONES_ROWS, blk), jnp.float32),
            pltpu.VMEM((grp * blk, blk), jnp.float32),
            pltpu.VMEM((grp * blk, blk), jnp.float32),
        ],
    )
    return pl.pallas_call(
        functools.partial(_moba_prompt_kernel, nblk, grp),
        out_shape=(jax.ShapeDtypeStruct((n, D_MODEL), out_dtype),
                   jax.ShapeDtypeStruct((n_phys, MOBA_HEADS, MOBA_HD), jnp.float32)),
        grid_spec=grid_spec,
        compiler_params=_cparams(("arbitrary", "arbitrary", "arbitrary")),
        name="moba_prompt",
    )(rel_tbl, q_arr, k_all, v_all, buckets, cache_k)


def _sample_select_kernel(n_pages, pt_ref, q_ref, ps_hbm, o_ref, ps_ref, sem):
    db = pl.program_id(0)
    nfull = n_pages // 2

    slot = lax.rem(db, 2)

    def page_copy(d, buf, j):
        dst = (j % 2) * nfull + j // 2
        return pltpu.make_async_copy(ps_hbm.at[pt_ref[d * n_pages + j]], ps_ref.at[buf, dst],
                                     sem.at[buf])

    def start_all(d, buf):
        def body(j, carry):
            page_copy(d, buf, j).start()
            return carry
        lax.fori_loop(0, n_pages, body, 0, unroll=ROW_DMA_UNROLL)

    def wait_all(d, buf):
        def body(j, carry):
            page_copy(d, buf, j).wait()
            return carry
        lax.fori_loop(0, n_pages, body, 0, unroll=ROW_DMA_UNROLL)

    @pl.when(db == 0)
    def _():
        start_all(db, slot)

    @pl.when(db + 1 < pl.num_programs(0))
    def _():
        start_all(db + 1, 1 - slot)

    wait_all(db, slot)

    nt = (((1,), (1,)), ((), ()))
    for h in range(MOBA_HEADS):
        means = (ps_ref[slot, pl.ds(0, nfull), h, :]
                 + ps_ref[slot, pl.ds(nfull, nfull), h, :]) * (1.0 / MOBA_BLOCK)
        mh = means.astype(jnp.bfloat16)
        ml = (means - mh.astype(jnp.float32)).astype(jnp.bfloat16)
        q = q_ref[:, h * MOBA_HD:(h + 1) * MOBA_HD].astype(jnp.bfloat16)
        scores = (lax.dot_general(q, mh, nt, preferred_element_type=jnp.float32)
                  + lax.dot_general(q, ml, nt, preferred_element_type=jnp.float32))
        col = lax.broadcasted_iota(jnp.int32, scores.shape, 1)
        sc = scores
        out = jnp.zeros((SAMPLE_ROWS, LANES), jnp.int32)
        lane = lax.broadcasted_iota(jnp.int32, out.shape, 1)
        for r in range(MOBA_TOPK):
            m = jnp.max(sc, axis=1, keepdims=True)
            idx = jnp.min(jnp.where(sc == m, col, nfull), axis=1, keepdims=True)
            out = jnp.where(lane == r, idx, out)
            sc = jnp.where(col == idx, -jnp.inf, sc)
        o_ref[0, h] = out


def _sample_select(page_table, q_arr, q_blk0, psums):
    ndb, n_pages = page_table.shape
    grid_spec = pltpu.PrefetchScalarGridSpec(
        num_scalar_prefetch=1,
        grid=(ndb,),
        in_specs=[
            pl.BlockSpec((SAMPLE_ROWS, D_MODEL), lambda d, pt: (d, q_blk0)),
            pl.BlockSpec(memory_space=pl.ANY),
        ],
        out_specs=pl.BlockSpec((1, MOBA_HEADS, SAMPLE_ROWS, LANES), lambda d, pt: (d, 0, 0, 0)),
        scratch_shapes=[pltpu.VMEM((2, n_pages, MOBA_HEADS, MOBA_HD), jnp.float32),
                        pltpu.SemaphoreType.DMA((2,))],
    )
    return pl.pallas_call(
        functools.partial(_sample_select_kernel, n_pages),
        out_shape=jax.ShapeDtypeStruct((ndb, MOBA_HEADS, SAMPLE_ROWS, LANES), jnp.int32),
        grid_spec=grid_spec,
        compiler_params=_cparams(("arbitrary",)),
        name="sample_select",
    )(page_table.reshape(-1), q_arr, psums)


def _sample_attn_kernel(n_pages, t_real, layer, pt_ref, idx_ref, tbl_ref, q_ref, k_ref, v_ref,
                        bk_ref, ck_hbm, cv_hbm, o_ref, ks_ref, vs_ref, sem):
    db = pl.program_id(0)
    h = pl.program_id(1)
    ppb = MOBA_BLOCK // PAGE_SIZE
    npage = MOBA_TOPK * ppb
    step = db * MOBA_HEADS + h
    n_steps = pl.num_programs(0) * MOBA_HEADS
    slot = lax.rem(step, 2)

    def copies(st, buf, t, j):
        d = st // MOBA_HEADS
        hh = lax.rem(st, MOBA_HEADS)
        blk = idx_ref[(st * SAMPLE_ROWS + t) * MOBA_TOPK + j // ppb]
        page = pt_ref[d * n_pages + blk * ppb + j % ppb]
        rows = pl.ds(j * PAGE_SIZE, PAGE_SIZE)
        return (pltpu.make_async_copy(ck_hbm.at[layer, page, :, hh, :], ks_ref.at[buf, t, rows],
                                      sem.at[buf, 0]),
                pltpu.make_async_copy(cv_hbm.at[layer, page, :, hh, :], vs_ref.at[buf, t, rows],
                                      sem.at[buf, 1]))

    def for_all_copies(st, buf, fn):
        for t in range(t_real):
            for j in range(npage):
                ck, cv = copies(st, buf, t, j)
                fn(ck)
                fn(cv)

    @pl.when(step == 0)
    def _():
        for_all_copies(step, slot, lambda c: c.start())

    @pl.when(step + 1 < n_steps)
    def _():
        for_all_copies(step + 1, 1 - slot, lambda c: c.start())

    for_all_copies(step, slot, lambda c: c.wait())

    nt = (((1,), (1,)), ((), ()))
    bf16 = jnp.bfloat16
    own_w = 2 * SAMPLE_ROWS
    q = q_ref[...].astype(bf16)
    pad = jnp.zeros((own_w - SAMPLE_ROWS, MOBA_HD), jnp.float32)
    kn = jnp.concatenate([k_ref[...], pad], axis=0).astype(bf16)
    vn = jnp.concatenate([v_ref[...], pad], axis=0).astype(bf16)
    far_bias = tbl_ref[h * REL_BUCKETS + REL_BUCKETS - 1]
    last_blk = n_pages // ppb - 1
    r_i = lax.broadcasted_iota(jnp.int32, (SAMPLE_ROWS, own_w), 0)
    c_i = lax.broadcasted_iota(jnp.int32, (SAMPLE_ROWS, own_w), 1)
    own_bias = _bias_from_buckets(jnp.maximum(r_i - c_i, 0), tbl_ref, h)
    near = _bias_from_buckets(bk_ref[...], tbl_ref, h)
    n_key = MOBA_TOPK * MOBA_BLOCK + own_w
    row_sel = lax.broadcasted_iota(jnp.int32, (SAMPLE_ROWS, n_key), 0)
    key_col = lax.broadcasted_iota(jnp.int32, (1, n_key), 1)
    out = jnp.zeros((SAMPLE_ROWS, MOBA_HD), jnp.float32)
    out_row = lax.broadcasted_iota(jnp.int32, out.shape, 0)
    for t in range(t_real):
        keys = jnp.concatenate([ks_ref[slot, t].astype(bf16), kn], axis=0)
        vals = jnp.concatenate([vs_ref[slot, t].astype(bf16), vn], axis=0)
        bias = []
        for r in range(MOBA_TOPK):
            blk = idx_ref[(step * SAMPLE_ROWS + t) * MOBA_TOPK + r]
            bias.append(jnp.where(blk == last_blk, near[t:t + 1, :], far_bias))
        bias.append(own_bias[t:t + 1, :])
        s_all = lax.dot_general(q, keys, nt, preferred_element_type=jnp.float32)
        s_t = jnp.sum(jnp.where(row_sel == t, s_all, 0.0), axis=0, keepdims=True)
        own_col = key_col - MOBA_TOPK * MOBA_BLOCK
        visible = (own_col < 0) | ((own_col <= t) & (own_col < t_real))
        s_t = jnp.where(visible, s_t + jnp.concatenate(bias, axis=1), NEG_INF)
        m = jnp.max(s_t, axis=1, keepdims=True)
        p = jnp.exp2(s_t - m)
        den = jnp.sum(p, axis=1, keepdims=True)
        p8 = jnp.broadcast_to(p, (SAMPLE_ROWS, n_key)).astype(bf16)
        o_t = jnp.dot(p8, vals, preferred_element_type=jnp.float32) / den
        out = jnp.where(out_row == t, o_t, out)
    o_ref[...] = out.astype(o_ref.dtype)


def _sample_attn(page_table, sel_idx, rel_tbl, q_arr, q_blk0, k_all, v_all, cache_k, cache_v,
                 layer, t_real, out_dtype):
    ndb, n_pages = page_table.shape
    n = k_all.shape[1]
    past = n_pages * PAGE_SIZE
    last0 = past - MOBA_BLOCK
    t = np.arange(SAMPLE_ROWS)
    r = np.arange(MOBA_BLOCK)
    near_buckets = jnp.asarray(_rel_bucket_np(past + t[:, None] - (last0 + r[None, :])))
    grid_spec = pltpu.PrefetchScalarGridSpec(
        num_scalar_prefetch=3,
        grid=(ndb, MOBA_HEADS),
        in_specs=[
            pl.BlockSpec((SAMPLE_ROWS, MOBA_HD), lambda d, h, *_: (d, q_blk0 + h)),
            pl.BlockSpec((None, SAMPLE_ROWS, MOBA_HD), lambda d, h, *_: (layer, d, h)),
            pl.BlockSpec((None, SAMPLE_ROWS, MOBA_HD), lambda d, h, *_: (layer, d, h)),
            pl.BlockSpec((SAMPLE_ROWS, MOBA_BLOCK), lambda d, h, *_: (0, 0)),
            pl.BlockSpec(memory_space=pl.ANY),
            pl.BlockSpec(memory_space=pl.ANY),
        ],
        out_specs=pl.BlockSpec((SAMPLE_ROWS, MOBA_HD), lambda d, h, *_: (d, h)),
        scratch_shapes=[
            pltpu.VMEM((2, t_real, MOBA_TOPK * MOBA_BLOCK, MOBA_HD), jnp.float32),
            pltpu.VMEM((2, t_real, MOBA_TOPK * MOBA_BLOCK, MOBA_HD), jnp.float32),
            pltpu.SemaphoreType.DMA((2, 2)),
        ],
    )
    return pl.pallas_call(
        functools.partial(_sample_attn_kernel, n_pages, t_real, layer),
        out_shape=jax.ShapeDtypeStruct((n, D_MODEL), out_dtype),
        grid_spec=grid_spec,
        compiler_params=_cparams(("arbitrary", "arbitrary")),
        name="sample_attn",
    )(page_table.reshape(-1), sel_idx, rel_tbl, q_arr, k_all, v_all, near_buckets, cache_k,
      cache_v)


def _layer_norm(x, g, b):
    mu = jnp.mean(x, axis=-1, keepdims=True)
    xc = x - mu
    var = jnp.mean(xc * xc, axis=-1, keepdims=True)
    return xc * lax.rsqrt(var + LN_EPS) * g + b


def _mix_kernel(r_ref, m_ref, ga_ref, gb_ref, x_ref, wr_ref, wm_ref, wo_ref, g_ref, b_ref,
                wh_ref, wl_ref, x1_ref, ri_ref, rg_ref, cnt_ref, carry_ref):
    i = pl.program_id(0)

    @pl.when(i == 0)
    def _():
        carry_ref[...] = jnp.zeros_like(carry_ref)

    f32 = jnp.float32
    a = jnp.dot(r_ref[...].astype(jnp.bfloat16), wr_ref[...], preferred_element_type=f32)
    m = jnp.dot(m_ref[...].astype(jnp.bfloat16), wm_ref[...], preferred_element_type=f32)
    mix = _sigmoid(ga_ref[...].astype(f32)) * a + _sigmoid(gb_ref[...].astype(f32)) * m
    mixed = jnp.dot(mix.astype(jnp.bfloat16), wo_ref[...], preferred_element_type=f32)
    x1 = _layer_norm(DN_ALPHA * x_ref[...] + mixed, g_ref[...], b_ref[...])
    x1_ref[...] = x1

    xh = x1.astype(jnp.bfloat16)
    xl = (x1 - xh.astype(f32)).astype(jnp.bfloat16)
    logits = (jnp.dot(xh, wh_ref[...], preferred_element_type=f32)
              + jnp.dot(xl, wh_ref[...], preferred_element_type=f32)
              + jnp.dot(xh, wl_ref[...], preferred_element_type=f32))
    lane = lax.broadcasted_iota(jnp.int32, logits.shape, 1)
    is_g = lane < MOE_GROUPS
    lgm = jnp.where(is_g, logits, -jnp.inf)
    mg = jnp.max(lgm, axis=1, keepdims=True)
    grp = jnp.min(jnp.where(lgm == mg, lane, LANES), axis=1, keepdims=True)
    pg = 1.0 / jnp.sum(jnp.where(is_g, jnp.exp(logits - mg), 0.0), axis=1, keepdims=True)
    e_lane = lane - MOE_GROUPS
    in_grp = (e_lane >= 0) & (e_lane < MOE_EXPERTS) & ((e_lane >> 3) == grp)
    v1 = jnp.where(in_grp, logits, -jnp.inf)
    t1 = jnp.max(v1, axis=1, keepdims=True)
    i1 = jnp.min(jnp.where(v1 == t1, lane, LANES), axis=1, keepdims=True)
    v2 = jnp.where(lane == i1, -jnp.inf, v1)
    t2 = jnp.max(v2, axis=1, keepdims=True)
    i2 = jnp.min(jnp.where(v2 == t2, lane, LANES), axis=1, keepdims=True)
    z = jnp.exp(t2 - t1)
    g1 = pg / (1.0 + z)
    g2 = pg * z / (1.0 + z)
    oh = jnp.where((lane == i1) | (lane == i2), 1.0, 0.0)
    tm = oh.shape[0]
    tri = jnp.where(lax.broadcasted_iota(jnp.int32, (tm, tm), 0)
                    > lax.broadcasted_iota(jnp.int32, (tm, tm), 1), 1.0, 0.0)
    cum = jnp.dot(tri.astype(jnp.bfloat16), oh.astype(jnp.bfloat16),
                  preferred_element_type=f32) + carry_ref[...]
    r1 = jnp.sum(jnp.where(lane == i1, cum, 0.0), axis=1, keepdims=True)
    r2 = jnp.sum(jnp.where(lane == i2, cum, 0.0), axis=1, keepdims=True)
    carry_ref[...] = carry_ref[...] + jnp.sum(oh, axis=0, keepdims=True)
    ri = jnp.where(lane == 0, i1 - MOE_GROUPS,
                   jnp.where(lane == 1, i2 - MOE_GROUPS,
                             jnp.where(lane == 2, r1.astype(jnp.int32),
                                       jnp.where(lane == 3, r2.astype(jnp.int32), 0))))
    ri_ref[...] = ri
    rg_ref[...] = jnp.where(lane == 0, g1, jnp.where(lane == 1, g2, 0.0))
    cnt_ref[...] = carry_ref[...]


def _mix(ret_act, moba_o, gates, x, w_ret_o, w_moba_o, w_out, ln_g, ln_b, w_hi, w_lo, tm):
    n = x.shape[0]
    full = lambda shape: pl.BlockSpec(shape, lambda i: (0, 0))
    return pl.pallas_call(
        _mix_kernel,
        out_shape=(jax.ShapeDtypeStruct((n, D_MODEL), jnp.float32),
                   jax.ShapeDtypeStruct((n, LANES), jnp.int32),
                   jax.ShapeDtypeStruct((n, LANES), jnp.float32),
                   jax.ShapeDtypeStruct((1, LANES), jnp.float32)),
        grid=(n // tm,),
        in_specs=[
            pl.BlockSpec((tm, RET_W), lambda i: (i, 0)),
            pl.BlockSpec((tm, D_MODEL), lambda i: (i, 0)),
            pl.BlockSpec((tm, D_MODEL), lambda i: (i, 0)),
            pl.BlockSpec((tm, D_MODEL), lambda i: (i, 1)),
            pl.BlockSpec((tm, D_MODEL), lambda i: (i, 0)),
            full((RET_W, D_MODEL)), full((D_MODEL, D_MODEL)), full((D_MODEL, D_MODEL)),
            full((1, D_MODEL)), full((1, D_MODEL)),
            full((D_MODEL, LANES)), full((D_MODEL, LANES)),
        ],
        out_specs=(pl.BlockSpec((tm, D_MODEL), lambda i: (i, 0)),
                   pl.BlockSpec((tm, LANES), lambda i: (i, 0)),
                   pl.BlockSpec((tm, LANES), lambda i: (i, 0)),
                   pl.BlockSpec((1, LANES), lambda i: (0, 0))),
        scratch_shapes=[pltpu.VMEM((1, LANES), jnp.float32)],
        compiler_params=_cparams(("arbitrary",)),
        name="mix_ln_router",
    )(ret_act, moba_o, gates, gates, x, w_ret_o, w_moba_o, w_out, ln_g, ln_b, w_hi, w_lo)


def _stage_dest(dest_hbm, dest_ref, sem):
    i = pl.program_id(0)
    slot = lax.rem(i, 2)

    def copy(step, buf):
        return pltpu.make_async_copy(dest_hbm.at[step], dest_ref.at[buf], sem.at[buf])

    @pl.when(i == 0)
    def _():
        copy(i, slot).start()

    @pl.when(i + 1 < pl.num_programs(0))
    def _():
        copy(i + 1, 1 - slot).start()

    copy(i, slot).wait()
    return slot


def _dest_at(dest_ref, slot, t, k):
    e = 2 * t + k
    return dest_ref[slot, lax.shift_right_logical(e, 7), lax.bitwise_and(e, LANES - 1)]


def _dispatch_kernel(tm, dest_hbm, x_ref, z_hbm, o_hbm, dest_ref, sem, isem):
    del z_hbm
    slot = _stage_dest(dest_hbm, dest_ref, isem)

    def row_copy(t, k):
        return pltpu.make_async_copy(x_ref.at[pl.ds(t, 1)],
                                     o_hbm.at[pl.ds(_dest_at(dest_ref, slot, t, k), 1)], sem)

    def start(t, carry):
        row_copy(t, 0).start()
        row_copy(t, 1).start()
        return carry

    def wait(t, carry):
        row_copy(t, 0).wait()
        row_copy(t, 1).wait()
        return carry

    lax.fori_loop(0, tm, start, 0, unroll=ROW_DMA_UNROLL)
    lax.fori_loop(0, tm, wait, 0, unroll=ROW_DMA_UNROLL)


def _dispatch(x1, dest, n_rows, tm):
    n = x1.shape[0]
    return pl.pallas_call(
        functools.partial(_dispatch_kernel, tm),
        out_shape=jax.ShapeDtypeStruct((n_rows, D_MODEL), jnp.float32),
        grid=(n // tm,),
        in_specs=[pl.BlockSpec(memory_space=pl.ANY),
                  pl.BlockSpec((tm, D_MODEL), lambda i: (i, 0)),
                  pl.BlockSpec(memory_space=pl.ANY)],
        out_specs=pl.BlockSpec(memory_space=pl.ANY),
        scratch_shapes=[pltpu.SMEM((2, 2 * tm // LANES, LANES), jnp.int32),
                        pltpu.SemaphoreType.DMA, pltpu.SemaphoreType.DMA((2,))],
        input_output_aliases={2: 0},
        compiler_params=_cparams(("arbitrary",)),
        name="moe_dispatch",
    )(dest, x1, jnp.zeros((n_rows, D_MODEL), jnp.float32))


def _experts_kernel(be_ref, nu_ref, x_ref, wg_ref, wu_ref, wd_ref, o_ref, wgb, wub, wdb):
    i = pl.program_id(0)
    used = i < nu_ref[0]
    first = (i == 0) | (be_ref[i] != be_ref[jnp.maximum(i - 1, 0)])

    @pl.when(used & first)
    def _():
        wgb[...] = wg_ref[...].astype(jnp.bfloat16)
        wub[...] = wu_ref[...].astype(jnp.bfloat16)
        wdb[...] = wd_ref[...].astype(jnp.bfloat16)

    @pl.when(used)
    def _():
        xb = x_ref[...].astype(jnp.bfloat16)
        g = jnp.dot(xb, wgb[...], preferred_element_type=jnp.float32)
        u = jnp.dot(xb, wub[...], preferred_element_type=jnp.float32)
        hid = (g * _sigmoid(g) * u).astype(jnp.bfloat16)
        o_ref[...] = jnp.dot(hid, wdb[...], preferred_element_type=jnp.float32)

    @pl.when(jnp.logical_not(used))
    def _():
        o_ref[...] = jnp.zeros_like(o_ref)


def _experts(xs, blk_exp, n_used, w_gate, w_up, w_down, layer):
    n_rows = xs.shape[0]
    r = MOE_ROWS
    grid_spec = pltpu.PrefetchScalarGridSpec(
        num_scalar_prefetch=2,
        grid=(n_rows // r,),
        in_specs=[
            pl.BlockSpec((r, D_MODEL), lambda i, be, nu: (i, 0)),
            pl.BlockSpec((None, None, D_MODEL, MOE_FF), lambda i, be, nu: (layer, be[i], 0, 0)),
            pl.BlockSpec((None, None, D_MODEL, MOE_FF), lambda i, be, nu: (layer, be[i], 0, 0)),
            pl.BlockSpec((None, None, MOE_FF, D_MODEL), lambda i, be, nu: (layer, be[i], 0, 0)),
        ],
        out_specs=pl.BlockSpec((r, D_MODEL), lambda i, be, nu: (i, 0)),
        scratch_shapes=[pltpu.VMEM((D_MODEL, MOE_FF), jnp.bfloat16),
                        pltpu.VMEM((D_MODEL, MOE_FF), jnp.bfloat16),
                        pltpu.VMEM((MOE_FF, D_MODEL), jnp.bfloat16)],
    )
    return pl.pallas_call(
        _experts_kernel,
        out_shape=jax.ShapeDtypeStruct((n_rows, D_MODEL), jnp.float32),
        grid_spec=grid_spec,
        compiler_params=_cparams(("arbitrary",)),
        name="moe_experts",
    )(blk_exp, n_used, xs, w_gate, w_up, w_down)


def _combine_kernel(tm, dest_hbm, x_ref, gt_ref, g_ref, b_ref, y_hbm, o_ref, ob_ref,
                    ya_ref, yb_ref, dest_ref, sem, isem):
    slot = _stage_dest(dest_hbm, dest_ref, isem)

    def row_copy(t, k):
        buf = ya_ref if k == 0 else yb_ref
        return pltpu.make_async_copy(y_hbm.at[pl.ds(_dest_at(dest_ref, slot, t, k), 1)],
                                     buf.at[pl.ds(t, 1)], sem.at[k])

    def start(t, carry):
        row_copy(t, 0).start()
        row_copy(t, 1).start()
        return carry

    def wait(t, carry):
        row_copy(t, 0).wait()
        row_copy(t, 1).wait()
        return carry

    lax.fori_loop(0, tm, start, 0, unroll=ROW_DMA_UNROLL)
    lax.fori_loop(0, tm, wait, 0, unroll=ROW_DMA_UNROLL)
    gt = gt_ref[...]
    y = gt[:, 0:1] * ya_ref[...] + gt[:, 1:2] * yb_ref[...]
    x2 = _layer_norm(DN_ALPHA * x_ref[...] + y, g_ref[...], b_ref[...])
    o_ref[...] = x2
    ob_ref[...] = x2.astype(jnp.bfloat16)


def _combine(x1, gates, dest, ys, ln_g, ln_b, tm):
    n = x1.shape[0]
    return pl.pallas_call(
        functools.partial(_combine_kernel, tm),
        out_shape=(jax.ShapeDtypeStruct((n, D_MODEL), jnp.float32),
                   jax.ShapeDtypeStruct((n, D_MODEL), jnp.bfloat16)),
        grid=(n // tm,),
        in_specs=[pl.BlockSpec(memory_space=pl.ANY),
                  pl.BlockSpec((tm, D_MODEL), lambda i: (i, 0)),
                  pl.BlockSpec((tm, LANES), lambda i: (i, 0)),
                  pl.BlockSpec((1, D_MODEL), lambda i: (0, 0)),
                  pl.BlockSpec((1, D_MODEL), lambda i: (0, 0)),
                  pl.BlockSpec(memory_space=pl.ANY)],
        out_specs=(pl.BlockSpec((tm, D_MODEL), lambda i: (i, 0)),
                   pl.BlockSpec((tm, D_MODEL), lambda i: (i, 0))),
        scratch_shapes=[pltpu.VMEM((tm, D_MODEL), jnp.float32),
                        pltpu.VMEM((tm, D_MODEL), jnp.float32),
                        pltpu.SMEM((2, 2 * tm // LANES, LANES), jnp.int32),
                        pltpu.SemaphoreType.DMA((2,)), pltpu.SemaphoreType.DMA((2,))],
        compiler_params=_cparams(("arbitrary",)),
        name="moe_combine_ln",
    )(dest, x1, gates, ln_g, ln_b, ys)


def _moe(x1, route_i, route_g, counts, w_gate, w_up, w_down, layer, ln_g, ln_b, tm):
    n = x1.shape[0]
    r = MOE_ROWS
    n_blk = (2 * n) // r + MOE_EXPERTS
    cnt = counts[0, MOE_GROUPS:MOE_GROUPS + MOE_EXPERTS].astype(jnp.int32)
    padded = (cnt + r - 1) // r * r
    pad_end = jnp.cumsum(padded)
    pad_start = pad_end - padded
    dest = (pad_start[route_i[:, 0:2]] + route_i[:, 2:4]).astype(jnp.int32)
    dest = dest.reshape(n // tm, 2 * tm // LANES, LANES)
    blk_start = jnp.arange(n_blk, dtype=jnp.int32) * r
    blk_exp = jnp.minimum(jnp.sum(pad_end[None, :] <= blk_start[:, None], axis=1),
                          MOE_EXPERTS - 1).astype(jnp.int32)
    n_used = (pad_end[-1:] // r).astype(jnp.int32)
    xs = _dispatch(x1, dest, n_blk * r, tm)
    ys = _experts(xs, blk_exp, n_used, w_gate, w_up, w_down, layer)
    return _combine(x1, route_g, dest, ys, ln_g, ln_b, tm)


def _split_hi_lo(w):
    hi = w.astype(jnp.bfloat16)
    return hi, (w - hi.astype(jnp.float32)).astype(jnp.bfloat16)


def _layer_common(x, xb, k_prev, v_prev, attn_fn, lw, layer, tm_proj, tm, tm_moe, act_dtype):
    (w_in, col_scale, w_ret_o, w_moba_o, w_out, ln1_g, ln1_b, w_hi, w_lo,
     w_gate, w_up, w_down, ln2_g, ln2_b) = lw
    proj = functools.partial(_proj, xb, w_in, layer, col_scale)
    qk = proj(COL_RQ, 2, jnp.float32, tm_proj)
    vgq = proj(COL_RV, 5, act_dtype, tm_proj)
    k_all = _proj_into(xb, w_in, layer, col_scale, COL_MK, k_prev, tm_proj)
    v_all = _proj_into(xb, w_in, layer, col_scale, COL_MK + 1, v_prev, tm_proj)
    gates = proj(COL_GA, 2, act_dtype, tm_proj)
    ret_act, ret_state, moba_o = attn_fn(qk, vgq, k_all, v_all)
    x1, route_i, route_g, counts = _mix(ret_act, moba_o, gates, x, w_ret_o, w_moba_o, w_out,
                                        ln1_g, ln1_b, w_hi, w_lo, tm)
    x2, x2b = _moe(x1, route_i, route_g, counts, w_gate, w_up, w_down, layer, ln2_g, ln2_b,
                   tm_moe)
    return x2, x2b, k_all, v_all, ret_state


def kernel(x_prompt, x_sample, cache_k, cache_v, state_ret, page_table, rel_bias, w_in, ret_gn_w,
           w_ret_o, w_moba_o, w_out, ln1_g, ln1_b, w_group, w_router, w_exp_gate, w_exp_up,
           w_exp_down, ln2_g, ln2_b):
    b, s, _ = x_prompt.shape
    db, t_real, _ = x_sample.shape
    n_pages = page_table.shape[1]
    past = n_pages * PAGE_SIZE
    bf16 = jnp.bfloat16

    col_scale = np.ones((1, PROJ_TILES * D_MODEL), np.float32)
    col_scale[0, D_MODEL:2 * D_MODEL] = RET_DK ** -0.5
    col_scale[0, COL_MQ * D_MODEL:(COL_MQ + 1) * D_MODEL] = MOBA_HD ** -0.5 * LOG2E
    col_scale = jnp.asarray(col_scale)
    rel_tbl = (rel_bias.T * LOG2E).reshape(-1)

    cos_p, sin_p = _xpos_tables(jnp.arange(s))
    pos_s = jnp.minimum(past + jnp.arange(SAMPLE_ROWS), past + t_real - 1)
    cos_s, sin_s = _xpos_tables(pos_s)

    xp = x_prompt.reshape(b * s, D_MODEL)
    xs = jnp.pad(x_sample, ((0, 0), (0, SAMPLE_ROWS - t_real), (0, 0))).reshape(
        db * SAMPLE_ROWS, D_MODEL)
    xpb, xsb = xp.astype(bf16), xs.astype(bf16)
    zero_state = jnp.zeros((b, RET_HEADS, RET_DK, RET_DV), jnp.float32)
    state_all = state_ret.reshape(DEPTH * db, RET_HEADS, RET_DK, RET_DV)
    mq_blk = (COL_MQ - COL_RV) * D_MODEL // MOBA_HD

    rp = []
    kp, vp = (jnp.zeros((DEPTH, b * s, D_MODEL), jnp.float32) for _ in range(2))
    ks, vs = (jnp.zeros((DEPTH, db * SAMPLE_ROWS, D_MODEL), jnp.float32) for _ in range(2))
    st_s = jnp.zeros((DEPTH * db, RET_HEADS, RET_DK, RET_DV), jnp.float32)
    for l in range(DEPTH):
        w_gr = jnp.zeros((D_MODEL, LANES), jnp.float32)
        w_gr = w_gr.at[:, :MOE_GROUPS].set(w_group[l])
        w_gr = w_gr.at[:, MOE_GROUPS:MOE_GROUPS + MOE_EXPERTS].set(w_router[l])
        w_hi, w_lo = _split_hi_lo(w_gr)
        lw = (w_in, col_scale, w_ret_o[l].astype(bf16), w_moba_o[l].astype(bf16),
              w_out[l].astype(bf16), ln1_g[l][None], ln1_b[l][None], w_hi, w_lo,
              w_exp_gate, w_exp_up, w_exp_down, ln2_g[l][None], ln2_b[l][None])
        gn = ret_gn_w[l][None]

        page_sums = []

        def prompt_attn(qk, vgq, k_all, v_all, l=l, page_sums=page_sums):
            rc = min(RET_STEP_ROWS, s)
            ret_act, st = _retention(qk, vgq, 0, 1, cos_p, sin_p, gn, zero_state, 0,
                                     b, s // rc, rc, rc, bf16)
            moba_o, psums = _moba_prompt(vgq, mq_blk, k_all, v_all, l, rel_tbl, cache_k, b, s,
                                         bf16)
            page_sums.append(psums)
            return ret_act, st, moba_o

        def sample_attn(qk, vgq, k_all, v_all, l=l, st_prev=st_s, page_sums=page_sums):
            ret_act, st = _retention(qk, vgq, 0, 1, cos_s, sin_s, gn, state_all, l * db,
                                     db, 1, SAMPLE_ROWS, t_real, jnp.float32,
                                     sf_rows=DEPTH * db, sf_off=l * db, sf_prev=st_prev)
            sel = _sample_select(page_table, vgq, COL_MQ - COL_RV, page_sums[0])
            sel_idx = sel[:, :, :, :MOBA_TOPK].reshape(-1)
            moba_o = _sample_attn(page_table, sel_idx, rel_tbl, vgq, mq_blk, k_all, v_all,
                                  cache_k, cache_v, l, t_real, jnp.float32)
            return ret_act, st, moba_o

        n_s = db * SAMPLE_ROWS
        xp, xpb, kp, vp, st_p = _layer_common(xp, xpb, kp, vp, prompt_attn, lw, l,
                                              min(1024, b * s), 512, 512, bf16)
        tm_s = min(256, n_s)
        xs, xsb, ks, vs, st_s = _layer_common(xs, xsb, ks, vs, sample_attn, lw, l, tm_s, tm_s,
                                              tm_s, jnp.float32)
        rp.append(st_p)

    page_shape = (DEPTH, b, s // PAGE_SIZE, PAGE_SIZE, MOBA_HEADS, MOBA_HD)
    new_shape = (DEPTH, db, SAMPLE_ROWS, MOBA_HEADS, MOBA_HD)
    yp = xp.reshape(b, s, D_MODEL)
    ys = xs.reshape(db, SAMPLE_ROWS, D_MODEL)[:, :t_real]
    return (yp, ys, kp.reshape(page_shape), vp.reshape(page_shape), jnp.stack(rp),
            ks.reshape(new_shape)[:, :, :t_real], vs.reshape(new_shape)[:, :, :t_real],
            st_s.reshape(DEPTH, db, RET_HEADS, RET_DK, RET_DV))
```
